```python
import jax, jax.numpy as jnp
from jax import lax
import numpy as np

D_MODEL = 2048
BATCH = 16
SEQ = 2048
DEPTH = 4

N_EVEN = (DEPTH + 1) // 2
N_ODD = DEPTH // 2

NSA_HEADS = 8
NSA_KV_GROUPS = 2
NSA_HEAD_DIM = 128
CMP_BLOCK = 32
CMP_STRIDE = 16
CMP_HIDDEN = 256
SEL_BLOCK = 64
SEL_TOPN = 8
SEL_QBLOCK = 32
WINDOW = 512
WIN_QBLOCK = 128
FORCE_BONUS = 1.0e4

HGRN_HEADS = 8
HGRN_KDIM = 128
HGRN_VDIM = 128
HGRN_CHUNK = 64
LB_MAX = 1.0 - 1e-6

CONV_CHANNELS = D_MODEL
CONV_WIDTH = 31

D_FF = 5632
N_EXPERTS = 8
TOP_K = 2
D_FF_EXPERT = 7168
MOE_BLOCK = 512

EPS = 1e-6
NEG_BIG = -1e30
TINY = 1e-30

NSA_Q = NSA_HEADS * NSA_HEAD_DIM
NSA_KV = NSA_KV_GROUPS * NSA_HEAD_DIM
HGRN_QK = HGRN_HEADS * HGRN_KDIM
HGRN_V = HGRN_HEADS * HGRN_VDIM
IN_SPLITS = (NSA_Q, 6 * NSA_KV, 3 * NSA_HEADS, HGRN_QK, HGRN_QK, HGRN_V, HGRN_V)
D_IN = sum(IN_SPLITS)
D_MIX = NSA_Q + HGRN_V

kernel_name = 'hybrid_nsa_hgrn2_conformer_moe'

f32 = jnp.float32


def rmsnorm(x, g):
    xf = x.astype(f32)
    y = xf * lax.rsqrt(jnp.mean(xf * xf, axis=-1, keepdims=True) + EPS)
    return (y * g.astype(f32)).astype(x.dtype)


def layernorm(x, g, b):
    xf = x.astype(f32)
    mu = jnp.mean(xf, axis=-1, keepdims=True)
    var = jnp.mean(jnp.square(xf - mu), axis=-1, keepdims=True)
    return ((xf - mu) * lax.rsqrt(var + EPS) * g.astype(f32) + b.astype(f32)).astype(x.dtype)


def masked_softmax(s, mask):
    s = jnp.where(mask, s, NEG_BIG)
    m = jnp.max(s, axis=-1, keepdims=True)
    e = jnp.where(mask, jnp.exp(s - m), 0.0)
    return e / jnp.maximum(jnp.sum(e, axis=-1, keepdims=True), TINY)


def alibi_slopes(n):
    return jnp.asarray(2.0 ** (-8.0 * np.arange(1, n + 1) / n), f32)


def swiglu(x, w1, w3, w2):
    return (jax.nn.silu(x @ w1) * (x @ w3)) @ w2


def cmp_sel_overlap(n_c, n_s):
    c0 = np.arange(n_c)[:, None] * CMP_STRIDE
    s0 = np.arange(n_s)[None, :] * SEL_BLOCK
    return ((c0 < s0 + SEL_BLOCK) & (c0 + CMP_BLOCK > s0)).astype(np.float32)


def compress_blocks(kv, pos_emb, w1, w2):
    B, T, G, Dh = kv.shape
    n_c = (T - CMP_BLOCK) // CMP_STRIDE + 1
    idx = np.arange(n_c)[:, None] * CMP_STRIDE + np.arange(CMP_BLOCK)[None, :]
    blocks = kv[:, idx] + pos_emb[:, None, :]
    blocks = blocks.transpose(0, 1, 3, 2, 4).reshape(B, n_c, G, CMP_BLOCK * Dh)
    return jax.nn.gelu(blocks @ w1) @ w2


def nsa_attention(q, k_cmp, v_cmp, k_sel, v_sel, k_win, v_win, gate_logits,
                  pos_k, pos_v, w1_k, w1_v, w2_k, w2_v):
    B, T, H, Dh = q.shape
    G = NSA_KV_GROUPS
    R = H // G
    dtype = q.dtype
    scale = Dh ** -0.5
    slopes = alibi_slopes(H).reshape(G, R)
    qg = q.reshape(B, T, G, R, Dh)
    pos = jnp.arange(T)

    n_c = (T - CMP_BLOCK) // CMP_STRIDE + 1
    kc = compress_blocks(k_cmp, pos_k, w1_k, w2_k)
    vc = compress_blocks(v_cmp, pos_v, w1_v, w2_v)
    c_end = jnp.arange(n_c) * CMP_STRIDE + CMP_BLOCK - 1
    c_dist = (pos[:, None] - c_end[None, :]).astype(f32)
    s = jnp.einsum('btgrd,bcgd->bgrtc', qg, kc).astype(f32) * scale
    s = s - slopes[:, :, None, None] * c_dist
    p_cmp = masked_softmax(s, c_dist >= 0)
    o_cmp = jnp.einsum('bgrtc,bcgd->btgrd', p_cmp.astype(dtype), vc)

    n_s = T // SEL_BLOCK
    n_sel = min(SEL_TOPN, n_s)
    imp = jnp.einsum('bgrtc,cs->bgts', p_cmp, jnp.asarray(cmp_sel_overlap(n_c, n_s)))
    blk = jnp.arange(n_s)[None, :]
    cur = (pos // SEL_BLOCK)[:, None]
    valid = blk * SEL_BLOCK <= pos[:, None]
    forced = valid & ((blk == 0) | (blk == cur) | (blk == cur - 1))
    score = jnp.where(valid, imp, -1.0) + jnp.where(forced, FORCE_BONUS, 0.0)
    _, sel_idx = lax.top_k(score, n_sel)

    ks_blk = k_sel.reshape(B, n_s, SEL_BLOCK, G, Dh).transpose(0, 3, 1, 2, 4)
    vs_blk = v_sel.reshape(B, n_s, SEL_BLOCK, G, Dh).transpose(0, 3, 1, 2, 4)
    nq = T // SEL_QBLOCK
    q_blocks = qg.reshape(B, nq, SEL_QBLOCK, G, R, Dh).swapaxes(0, 1)
    idx_blocks = sel_idx.reshape(B, G, nq, SEL_QBLOCK, n_sel).transpose(2, 0, 1, 3, 4)
    gather = jax.vmap(jax.vmap(lambda blocks, ix: blocks[ix]))

    def sel_block(args):
        qb, ib, q0 = args
        kg = gather(ks_blk, ib)
        vg = gather(vs_blk, ib)
        qpos = q0 + jnp.arange(SEL_QBLOCK)
        kpos = ib[..., None] * SEL_BLOCK + jnp.arange(SEL_BLOCK)
        dist = (qpos[None, None, :, None, None] - kpos).astype(f32)
        dist = dist.reshape(B, G, 1, SEL_QBLOCK, n_sel * SEL_BLOCK)
        sb = jnp.einsum('bqgrd,bgqnsd->bgrqns', qb, kg).astype(f32) * scale
        sb = sb.reshape(B, G, R, SEL_QBLOCK, n_sel * SEL_BLOCK) - slopes[None, :, :, None, None] * dist
        p = masked_softmax(sb, dist >= 0).reshape(B, G, R, SEL_QBLOCK, n_sel, SEL_BLOCK)
        return jnp.einsum('bgrqns,bgqnsd->bqgrd', p.astype(dtype), vg)

    o_sel = lax.map(sel_block, (q_blocks, idx_blocks, jnp.arange(nq) * SEL_QBLOCK))
    o_sel = o_sel.swapaxes(0, 1).reshape(B, T, G, R, Dh)

    span = WINDOW + WIN_QBLOCK
    kwp = jnp.pad(k_win, ((0, 0), (WINDOW, 0), (0, 0), (0, 0)))
    vwp = jnp.pad(v_win, ((0, 0), (WINDOW, 0), (0, 0), (0, 0)))
    nqw = T // WIN_QBLOCK
    qw_blocks = qg.reshape(B, nqw, WIN_QBLOCK, G, R, Dh).swapaxes(0, 1)

    def win_block(args):
        qb, q0 = args
        kb = lax.dynamic_slice_in_dim(kwp, q0, span, axis=1)
        vb = lax.dynamic_slice_in_dim(vwp, q0, span, axis=1)
        qpos = q0 + jnp.arange(WIN_QBLOCK)
        kpos = q0 - WINDOW + jnp.arange(span)
        dist = (qpos[:, None] - kpos[None, :]).astype(f32)
        mask = (dist >= 0) & (dist < WINDOW) & (kpos[None, :] >= 0)
        sw = jnp.einsum('bqgrd,bkgd->bgrqk', qb, kb).astype(f32) * scale
        sw = sw - slopes[:, :, None, None] * dist
        p = masked_softmax(sw, mask)
        return jnp.einsum('bgrqk,bkgd->bqgrd', p.astype(dtype), vb)

    o_win = lax.map(win_block, (qw_blocks, jnp.arange(nqw) * WIN_QBLOCK))
    o_win = o_win.swapaxes(0, 1).reshape(B, T, G, R, Dh)

    gates = jax.nn.sigmoid(gate_logits.astype(f32)).astype(dtype).reshape(B, T, G, R, 3)
    o = gates[..., 0:1] * o_cmp + gates[..., 1:2] * o_sel + gates[..., 2:3] * o_win
    return o.reshape(B, T, H * Dh)


def hgrn_lower_bounds_per_layer(table):
    p = jax.nn.softmax(table.astype(f32), axis=0)
    return jnp.cumsum(p, axis=0) - p[0]


def hgrn2(q, f_logit, i, g, lower_bound, norm_g):
    B, T, H, K = q.shape
    V = i.shape[-1]
    dtype = i.dtype
    C = HGRN_CHUNK
    n = T // C
    z = f_logit.astype(f32)
    lb = jnp.clip(lower_bound.astype(f32), 0.0, LB_MAX).reshape(H, K)
    log_f = jnp.logaddexp(jnp.log(jnp.maximum(lb, TINY)), jnp.log1p(-lb) + jax.nn.log_sigmoid(z))
    k = (1.0 - lb) * jax.nn.sigmoid(-z)
    qf = jax.nn.silu(q.astype(f32))
    v = i.astype(f32)

    def chunks(a):
        return a.reshape(B, n, C, H, a.shape[-1]).transpose(1, 0, 3, 2, 4)

    causal = jnp.tril(jnp.ones((C, C), bool))

    def step(S, xs):
        qc, kc, vc, lc = xs
        b = jnp.cumsum(lc, axis=2)
        o_inter = jnp.einsum('bhtk,bhkv->bhtv', qc * jnp.exp(b), S)
        diff = b[:, :, :, None, :] - b[:, :, None, :, :]
        decay = jnp.exp(jnp.where(causal[:, :, None], diff, NEG_BIG))
        A = jnp.einsum('bhtsk,bhsk->bhts', qc[:, :, :, None, :] * decay, kc)
        o = o_inter + jnp.einsum('bhts,bhsv->bhtv', A, vc)
        b_last = b[:, :, -1]
        S = jnp.exp(b_last)[..., None] * S + jnp.einsum(
            'bhsk,bhsv->bhkv', kc * jnp.exp(b_last[:, :, None] - b), vc)
        return S, o

    S0 = jnp.zeros((B, H, K, V), f32)
    _, o = lax.scan(step, S0, (chunks(qf), chunks(k), chunks(v), chunks(log_f)))
    o = o.transpose(1, 0, 3, 2, 4).reshape(B, T, H, V)
    o = o * lax.rsqrt(jnp.mean(o * o, axis=-1, keepdims=True) + EPS) * norm_g.astype(f32)
    o = o * jax.nn.silu(g.astype(f32))
    return o.astype(dtype).reshape(B, T, H * V)


def mixer_ab(u, w_in, w_out, pos_k, pos_v, w1_k, w1_v, w2_k, w2_v, lower_bound, norm_g):
    B, T, _ = u.shape
    proj = u @ w_in
    q_a, kv_a, gate_a, q_b, f_b, i_b, g_b = jnp.split(proj, list(np.cumsum(IN_SPLITS)[:-1]), axis=-1)
    q_a = q_a.reshape(B, T, NSA_HEADS, NSA_HEAD_DIM)
    kv_a = kv_a.reshape(B, T, 6, NSA_KV_GROUPS, NSA_HEAD_DIM)
    gate_a = gate_a.reshape(B, T, NSA_HEADS, 3)
    o_a = nsa_attention(q_a, kv_a[:, :, 0], kv_a[:, :, 1], kv_a[:, :, 2], kv_a[:, :, 3],
                        kv_a[:, :, 4], kv_a[:, :, 5], gate_a, pos_k, pos_v, w1_k, w1_v, w2_k, w2_v)
    o_b = hgrn2(q_b.reshape(B, T, HGRN_HEADS, HGRN_KDIM), f_b.reshape(B, T, HGRN_HEADS, HGRN_KDIM),
                i_b.reshape(B, T, HGRN_HEADS, HGRN_VDIM), g_b.reshape(B, T, HGRN_HEADS, HGRN_VDIM),
                lower_bound, norm_g)
    return jnp.concatenate([o_a, o_b], axis=-1) @ w_out


def conformer_conv(u, w_pw1, b_pw1, w_dw, b_dw, ln_g, ln_b, w_pw2, b_pw2):
    h = u @ w_pw1 + b_pw1
    a, gt = jnp.split(h, 2, axis=-1)
    h = a * jax.nn.sigmoid(gt)
    h = lax.conv_general_dilated(h, w_dw[:, None, :], window_strides=(1,),
                                 padding=[(CONV_WIDTH - 1, 0)],
                                 dimension_numbers=('NWC', 'WIO', 'NWC'),
                                 feature_group_count=CONV_CHANNELS) + b_dw
    h = jax.nn.silu(layernorm(h, ln_g, ln_b))
    return h @ w_pw2 + b_pw2


def moe_swiglu(x, w_router, w1, w3, w2):
    N, D = x.shape
    E = N_EXPERTS
    logits = (x @ w_router).astype(f32)
    top_logits, top_idx = lax.top_k(logits, TOP_K)
    gates = jax.nn.softmax(top_logits, axis=-1)
    NK = N * TOP_K
    flat_e = top_idx.reshape(NK)
    flat_tok = jnp.repeat(jnp.arange(N, dtype=jnp.int32), TOP_K)
    flat_g = gates.reshape(NK)
    order = jnp.argsort(flat_e)
    e_sorted = flat_e[order]
    counts = jnp.bincount(flat_e, length=E)
    padded = (counts + MOE_BLOCK - 1) // MOE_BLOCK * MOE_BLOCK
    start = jnp.cumsum(counts) - counts
    pstart = jnp.cumsum(padded) - padded
    dest = pstart[e_sorted] + jnp.arange(NK) - start[e_sorted]
    P = ((NK + E * MOE_BLOCK + MOE_BLOCK - 1) // MOE_BLOCK) * MOE_BLOCK
    tok_buf = jnp.zeros((P,), jnp.int32).at[dest].set(flat_tok[order])
    gate_buf = jnp.zeros((P,), f32).at[dest].set(flat_g[order])
    n_blk = P // MOE_BLOCK
    blk_e = jnp.minimum(jnp.searchsorted(jnp.cumsum(padded), jnp.arange(n_blk) * MOE_BLOCK, side='right'), E - 1)
    xb = x[tok_buf].reshape(n_blk, MOE_BLOCK, D)

    def expert_block(args):
        xs, e = args
        return swiglu(xs, w1[e], w3[e], w2[e])

    yb = lax.map(expert_block, (xb, blk_e)).reshape(P, D)
    return jnp.zeros_like(x).at[tok_buf].add(yb * gate_buf[:, None].astype(x.dtype))


def setup_inputs(seed: int = 0) -> dict:
    key = jax.random.key(seed)
    ks = iter(jax.random.split(key, 32))

    def nrm(shape, scale):
        return jax.random.normal(next(ks), shape, f32) * scale

    def gain(shape):
        return 1.0 + nrm(shape, 0.02)

    Dh = NSA_HEAD_DIM
    C = CONV_CHANNELS
    return {
        'x': nrm((BATCH, SEQ, D_MODEL), 1.0),
        'norm_mix': gain((DEPTH, D_MODEL)),
        'norm_ffn': gain((DEPTH, D_MODEL)),
        'final_norm': gain((D_MODEL,)),
        'ab_w_in': nrm((N_EVEN, D_MODEL, D_IN), D_MODEL ** -0.5),
        'ab_w_out': nrm((N_EVEN, D_MIX, D_MODEL), D_MIX ** -0.5),
        'cmp_pos_k': nrm((N_EVEN, CMP_BLOCK, Dh), 0.02),
        'cmp_pos_v': nrm((N_EVEN, CMP_BLOCK, Dh), 0.02),
        'cmp_w1_k': nrm((N_EVEN, CMP_BLOCK * Dh, CMP_HIDDEN), (CMP_BLOCK * Dh) ** -0.5),
        'cmp_w1_v': nrm((N_EVEN, CMP_BLOCK * Dh, CMP_HIDDEN), (CMP_BLOCK * Dh) ** -0.5),
        'cmp_w2_k': nrm((N_EVEN, CMP_HIDDEN, Dh), CMP_HIDDEN ** -0.5),
        'cmp_w2_v': nrm((N_EVEN, CMP_HIDDEN, Dh), CMP_HIDDEN ** -0.5),
        'hgrn_lower_bounds': nrm((N_EVEN, HGRN_QK), 1.0),
        'hgrn_norm': gain((N_EVEN, HGRN_VDIM)),
        'ffn_w1': nrm((N_EVEN, D_MODEL, D_FF), D_MODEL ** -0.5),
        'ffn_w3': nrm((N_EVEN, D_MODEL, D_FF), D_MODEL ** -0.5),
        'ffn_w2': nrm((N_EVEN, D_FF, D_MODEL), D_FF ** -0.5),
        'conv_w_pw1': nrm((N_ODD, D_MODEL, 2 * C), D_MODEL ** -0.5),
        'conv_b_pw1': nrm((N_ODD, 2 * C), 0.01),
        'conv_w_dw': nrm((N_ODD, CONV_WIDTH, C), CONV_WIDTH ** -0.5),
        'conv_b_dw': nrm((N_ODD, C), 0.01),
        'conv_ln_g': gain((N_ODD, C)),
        'conv_ln_b': nrm((N_ODD, C), 0.01),
        'conv_w_pw2': nrm((N_ODD, C, D_MODEL), C ** -0.5),
        'conv_b_pw2': nrm((N_ODD, D_MODEL), 0.01),
        'moe_router': nrm((N_ODD, D_MODEL, N_EXPERTS), D_MODEL ** -0.5),
        'moe_w1': nrm((N_ODD, N_EXPERTS, D_MODEL, D_FF_EXPERT), D_MODEL ** -0.5),
        'moe_w3': nrm((N_ODD, N_EXPERTS, D_MODEL, D_FF_EXPERT), D_MODEL ** -0.5),
        'moe_w2': nrm((N_ODD, N_EXPERTS, D_FF_EXPERT, D_MODEL), D_FF_EXPERT ** -0.5),
    }


def reference(x, norm_mix, norm_ffn, final_norm, ab_w_in, ab_w_out,
              cmp_pos_k, cmp_pos_v, cmp_w1_k, cmp_w1_v, cmp_w2_k, cmp_w2_v,
              hgrn_lower_bounds, hgrn_norm, ffn_w1, ffn_w3, ffn_w2,
              conv_w_pw1, conv_b_pw1, conv_w_dw, conv_b_dw, conv_ln_g, conv_ln_b,
              conv_w_pw2, conv_b_pw2, moe_router, moe_w1, moe_w3, moe_w2):
    B, T, D = x.shape
    lbs = hgrn_lower_bounds_per_layer(hgrn_lower_bounds)
    h = x
    for layer in range(DEPTH):
        j = layer // 2
        u = rmsnorm(h, norm_mix[layer])
        if layer % 2 == 0:
            h = h + mixer_ab(u, ab_w_in[j], ab_w_out[j], cmp_pos_k[j], cmp_pos_v[j],
                             cmp_w1_k[j], cmp_w1_v[j], cmp_w2_k[j], cmp_w2_v[j],
                             lbs[j], hgrn_norm[j])
        else:
            h = h + conformer_conv(u, conv_w_pw1[j], conv_b_pw1[j], conv_w_dw[j], conv_b_dw[j],
                                   conv_ln_g[j], conv_ln_b[j], conv_w_pw2[j], conv_b_pw2[j])
        u = rmsnorm(h, norm_ffn[layer])
        if layer % 2 == 0:
            h = h + swiglu(u, ffn_w1[j], ffn_w3[j], ffn_w2[j])
        else:
            h = h + moe_swiglu(u.reshape(B * T, D), moe_router[j], moe_w1[j],
                               moe_w3[j], moe_w2[j]).reshape(B, T, D)
    return rmsnorm(h, final_norm)
```

```python
import functools

import numpy as np
import jax
import jax.numpy as jnp
from jax import lax
from jax.experimental import pallas as pl
from jax.experimental.pallas import tpu as pltpu

f32 = jnp.float32
bf16 = jnp.bfloat16

NSA_HEADS = 8
NSA_KV_GROUPS = 2
NSA_HEAD_DIM = 128
NSA_REP = NSA_HEADS // NSA_KV_GROUPS
CMP_BLOCK = 32
CMP_STRIDE = 16
CMP_HIDDEN = 256
SEL_BLOCK = 64
SEL_TOPN = 8
WINDOW = 512
FORCE_BONUS = 1.0e4
HGRN_HEADS = 8
HGRN_DIM = 128
LB_MAX = 1.0 - 1e-6
CONV_WIDTH = 31
N_EXPERTS = 8
EPS = 1e-6
NEG_BIG = -1e30
TINY = 1e-30
LOWEST = -3.0e38

LANES = 128
SUBLANES = 8
VMEM_LIMIT = 56 * 1024 * 1024

HGRN_CHUNK = 64
ATT_TQ = 128
ATT_TK = 128
CONV_TT = 256
CONV_HALO = 32
MOE_TM = 512
ROUTER_TM = 512
ROW_TILE = 256


def _cparams(*sem):
    return pltpu.CompilerParams(dimension_semantics=sem, vmem_limit_bytes=VMEM_LIMIT)


def _dot(a, b):
    return jnp.dot(a, b, preferred_element_type=f32)


def _dot_nt(a, b):
    return lax.dot_general(a, b, (((1,), (1,)), ((), ())), preferred_element_type=f32)


def _dot_tn(a, b):
    return lax.dot_general(a, b, (((0,), (0,)), ((), ())), preferred_element_type=f32)


def _sigmoid(x):
    return 1.0 / (1.0 + jnp.exp(-x))


def _silu(x):
    return x * _sigmoid(x)


def _split3(x):
    hi = x.astype(bf16)
    r = x - hi.astype(f32)
    mid = r.astype(bf16)
    lo = (r - mid.astype(f32)).astype(bf16)
    return hi, mid, lo


def _rmsnorm_kernel(x_ref, g_ref, o_ref):
    x = x_ref[...]
    y = x * lax.rsqrt(jnp.mean(x * x, axis=-1, keepdims=True) + EPS) * g_ref[...]
    o_ref[...] = y.astype(o_ref.dtype)


def rmsnorm(x, g, out_dtype, tm=ROW_TILE):
    m, d = x.shape
    return pl.pallas_call(
        _rmsnorm_kernel,
        grid=(m // tm,),
        in_specs=[pl.BlockSpec((tm, d), lambda i: (i, 0)),
                  pl.BlockSpec((1, d), lambda i: (0, 0))],
        out_specs=pl.BlockSpec((tm, d), lambda i: (i, 0)),
        out_shape=jax.ShapeDtypeStruct((m, d), out_dtype),
        compiler_params=_cparams("parallel"),
        name="rmsnorm",
    )(x, g.reshape(1, d))


def _mm_kernel(be_ref, nu_ref, x_ref, w_ref, *rest, has_bias, has_res):
    rest = list(rest)
    b_ref = rest.pop(0) if has_bias else None
    r_ref = rest.pop(0) if has_res else None
    o_ref = rest.pop(0)
    i = pl.program_id(0)

    @pl.when(i < nu_ref[0])
    def _():
        acc = _dot(x_ref[...], w_ref[...])
        if has_bias:
            acc = acc + b_ref[...]
        if has_res:
            acc = acc + r_ref[...]
        o_ref[...] = acc.astype(o_ref.dtype)

    @pl.when(i >= nu_ref[0])
    def _():
        o_ref[...] = jnp.zeros(o_ref.shape, o_ref.dtype)


def matmul(x, w, *, bias=None, residual=None, out_dtype=f32, tm=1024, tn=512,
           blk_e=None, n_used=None):
    m, k = x.shape
    n = w.shape[-1]
    tm = min(tm, m)
    tn = min(tn, n)
    assert m % tm == 0 and n % tn == 0, (m, n, tm, tn)
    nblk = m // tm
    if blk_e is None:
        blk_e = jnp.zeros((nblk,), jnp.int32)
        n_used = jnp.full((1,), nblk, jnp.int32)

    def jj(i, j, nu):
        return jnp.where(i < nu[0], j, 0)

    in_specs = [pl.BlockSpec((tm, k), lambda i, j, be, nu: (i, 0)),
                pl.BlockSpec((None, k, tn), lambda i, j, be, nu: (be[i], 0, jj(i, j, nu)))]
    args = [x, w]
    if bias is not None:
        in_specs.append(pl.BlockSpec((1, tn), lambda i, j, be, nu: (0, j)))
        args.append(bias)
    if residual is not None:
        in_specs.append(pl.BlockSpec((tm, tn), lambda i, j, be, nu: (i, j)))
        args.append(residual)
    kern = functools.partial(_mm_kernel, has_bias=bias is not None, has_res=residual is not None)
    return pl.pallas_call(
        kern,
        grid_spec=pltpu.PrefetchScalarGridSpec(
            num_scalar_prefetch=2, grid=(nblk, n // tn), in_specs=in_specs,
            out_specs=pl.BlockSpec((tm, tn), lambda i, j, be, nu: (i, j))),
        out_shape=jax.ShapeDtypeStruct((m, n), out_dtype),
        compiler_params=_cparams("parallel", "arbitrary"),
        name="matmul",
    )(blk_e, n_used, *args)


def _mm2_kernel(be_ref, nu_ref, x_ref, wa_ref, wb_ref, *rest, has_bias, act_a, act_b):
    rest = list(rest)
    ba_ref = rest.pop(0) if has_bias else None
    bb_ref = rest.pop(0) if has_bias else None
    o_ref = rest.pop(0)
    i = pl.program_id(0)

    @pl.when(i < nu_ref[0])
    def _():
        x = x_ref[...]
        a = _dot(x, wa_ref[...])
        b = _dot(x, wb_ref[...])
        if has_bias:
            a = a + ba_ref[...]
            b = b + bb_ref[...]
        if act_a == "silu":
            a = _silu(a)
        if act_b == "sigmoid":
            b = _sigmoid(b)
        o_ref[...] = (a * b).astype(o_ref.dtype)

    @pl.when(i >= nu_ref[0])
    def _():
        o_ref[...] = jnp.zeros(o_ref.shape, o_ref.dtype)


def matmul_gated(x, wa, wb, *, ba=None, bb=None, act_a=None, act_b=None, out_dtype=bf16,
                 tm=1024, tn=512, blk_e=None, n_used=None):
    m, k = x.shape
    n = wa.shape[-1]
    tm = min(tm, m)
    tn = min(tn, n)
    assert m % tm == 0 and n % tn == 0, (m, n, tm, tn)
    nblk = m // tm
    if blk_e is None:
        blk_e = jnp.zeros((nblk,), jnp.int32)
        n_used = jnp.full((1,), nblk, jnp.int32)

    def wmap(i, j, be, nu):
        return (be[i], 0, jnp.where(i < nu[0], j, 0))

    in_specs = [pl.BlockSpec((tm, k), lambda i, j, be, nu: (i, 0)),
                pl.BlockSpec((None, k, tn), wmap),
                pl.BlockSpec((None, k, tn), wmap)]
    args = [x, wa, wb]
    if ba is not None:
        in_specs += [pl.BlockSpec((1, tn), lambda i, j, be, nu: (0, j))] * 2
        args += [ba, bb]
    kern = functools.partial(_mm2_kernel, has_bias=ba is not None, act_a=act_a, act_b=act_b)
    return pl.pallas_call(
        kern,
        grid_spec=pltpu.PrefetchScalarGridSpec(
            num_scalar_prefetch=2, grid=(nblk, n // tn), in_specs=in_specs,
            out_specs=pl.BlockSpec((tm, tn), lambda i, j, be, nu: (i, j))),
        out_shape=jax.ShapeDtypeStruct((m, n), out_dtype),
        compiler_params=_cparams("parallel", "arbitrary"),
        name="matmul_gated",
    )(blk_e, n_used, *args)


def _gelu_tanh(x):
    c = np.sqrt(2.0 / np.pi).astype(np.float32)
    return 0.5 * x * (1.0 + jnp.tanh(c * (x + 0.044715 * (x * x * x))))


def _compress_kernel(x_ref, pos_ref, w1_ref, w2_ref, o_ref, *, n_c):
    half = CMP_STRIDE * NSA_HEAD_DIM
    x = x_ref[...].astype(f32)
    rows = x.shape[0]
    xn = pltpu.roll(x, rows - 1, 0)
    pos = pos_ref[...]
    xa = (x + pos[:, :half]).astype(bf16)
    xb = (xn + pos[:, half:]).astype(bf16)
    h = _dot(xa, w1_ref[:half, :]) + _dot(xb, w1_ref[half:, :])
    o = _dot(_gelu_tanh(h).astype(bf16), w2_ref[...])
    ridx = lax.broadcasted_iota(jnp.int32, o.shape, 0)
    o_ref[...] = jnp.where(ridx < n_c, o, 0.0).astype(o_ref.dtype)


def nsa_compress(xg, pos, w1, w2, n_c):
    b, _, g, rows, width = xg.shape
    kern = functools.partial(_compress_kernel, n_c=n_c)
    return pl.pallas_call(
        kern,
        grid=(b, 2, g),
        in_specs=[pl.BlockSpec((None, None, None, rows, width), lambda i, s, j: (i, s, j, 0, 0)),
                  pl.BlockSpec((None, 1, 2 * width), lambda i, s, j: (s, 0, 0)),
                  pl.BlockSpec((None, 2 * width, CMP_HIDDEN), lambda i, s, j: (s, 0, 0)),
                  pl.BlockSpec((None, CMP_HIDDEN, NSA_HEAD_DIM), lambda i, s, j: (s, 0, 0))],
        out_specs=pl.BlockSpec((None, None, None, rows, NSA_HEAD_DIM),
                               lambda i, s, j: (i, s, j, 0, 0)),
        out_shape=jax.ShapeDtypeStruct((b, 2, g, rows, NSA_HEAD_DIM), bf16),
        compiler_params=_cparams("parallel", "parallel", "parallel"),
        name="nsa_compress",
    )(xg, pos, w1, w2)


def _head_slopes(rows_head, g):
    h = g * NSA_REP + rows_head
    out = jnp.zeros(h.shape, f32)
    for hh in range(NSA_HEADS):
        out = jnp.where(h == hh, np.float32(2.0 ** (-8.0 * (hh + 1) / NSA_HEADS)), out)
    return out


def _cmp_kernel(q_ref, kc_ref, vc_ref, ov_ref, o_ref, sel_ref, *, n_c, n_s, tq):
    g = pl.program_id(1)
    q0 = pl.program_id(2) * tq
    scale = np.float32(NSA_HEAD_DIM ** -0.5)
    tpos = q0 + lax.broadcasted_iota(jnp.int32, (tq, 1), 0)
    lane = lax.broadcasted_iota(jnp.int32, (1, LANES), 1)
    c_dist = (tpos - (lane * CMP_STRIDE + (CMP_BLOCK - 1))).astype(f32)
    cmask = (c_dist >= 0) & (lane < n_c)
    kc = kc_ref[...]
    vc = vc_ref[...]
    psum = jnp.zeros((tq, LANES), f32)
    for r in range(NSA_REP):
        slope = _head_slopes(jnp.full((1, 1), r, jnp.int32), g)
        qr = q_ref[:, r * NSA_HEAD_DIM:(r + 1) * NSA_HEAD_DIM]
        s = _dot_nt(qr, kc) * scale - slope * c_dist
        s = jnp.where(cmask, s, NEG_BIG)
        m = jnp.max(s, axis=-1, keepdims=True)
        e = jnp.where(cmask, jnp.exp(s - m), 0.0)
        p = e / jnp.maximum(jnp.sum(e, axis=-1, keepdims=True), TINY)
        o_ref[:, r * NSA_HEAD_DIM:(r + 1) * NSA_HEAD_DIM] = _dot(p.astype(bf16), vc).astype(o_ref.dtype)
        psum = psum + p
    hi, mid, lo = _split3(psum)
    ov = ov_ref[...]
    imp = _dot(hi, ov) + _dot(mid, ov) + _dot(lo, ov)
    cur = tpos // SEL_BLOCK
    valid = lane * SEL_BLOCK <= tpos
    forced = valid & ((lane == 0) | (lane == cur) | (lane == cur - 1))
    score = jnp.where(valid, imp, -1.0) + jnp.where(forced, FORCE_BONUS, 0.0)
    score = jnp.where(lane < n_s, score, LOWEST)
    lane_f = lane.astype(f32)
    sel = jnp.zeros((tq, LANES), f32)
    for _ in range(min(SEL_TOPN, n_s)):
        m = jnp.max(score, axis=-1, keepdims=True)
        first = jnp.min(jnp.where(score == m, lane_f, 1e9), axis=-1, keepdims=True)
        hit = lane_f == first
        sel = jnp.where(hit, 1.0, sel)
        score = jnp.where(hit, LOWEST, score)
    sel_ref[...] = sel.astype(sel_ref.dtype)


def nsa_cmp_attention(q, kvc, overlap, n_c, n_s, tq=ROW_TILE):
    b, t, _ = q.shape
    g = NSA_KV_GROUPS
    ncp = kvc.shape[3]
    assert ncp == LANES, "compressed blocks are laid out on one lane tile"
    gw = NSA_REP * NSA_HEAD_DIM
    kern = functools.partial(_cmp_kernel, n_c=n_c, n_s=n_s, tq=tq)
    return pl.pallas_call(
        kern,
        grid=(b, g, t // tq),
        in_specs=[pl.BlockSpec((None, tq, gw), lambda i, j, k: (i, k, j)),
                  pl.BlockSpec((None, None, None, ncp, NSA_HEAD_DIM), lambda i, j, k: (i, 0, j, 0, 0)),
                  pl.BlockSpec((None, None, None, ncp, NSA_HEAD_DIM), lambda i, j, k: (i, 1, j, 0, 0)),
                  pl.BlockSpec((LANES, LANES), lambda i, j, k: (0, 0))],
        out_specs=[pl.BlockSpec((None, tq, gw), lambda i, j, k: (i, k, j)),
                   pl.BlockSpec((None, None, tq, LANES), lambda i, j, k: (i, j, k, 0))],
        out_shape=[jax.ShapeDtypeStruct((b, t, g * gw), bf16),
                   jax.ShapeDtypeStruct((b, g, t, LANES), bf16)],
        compiler_params=_cparams("parallel", "parallel", "parallel"),
        name="nsa_cmp",
    )(q, kvc, kvc, overlap)


def _selwin_kernel(q_ref, ks_ref, vs_ref, kw_ref, vw_ref, sel_ref, oc_ref, gt_ref, o_ref, *, tq, tk):
    g = pl.program_id(1)
    qi = pl.program_id(2)
    q0 = qi * tq
    rep = NSA_REP
    dh = NSA_HEAD_DIM
    scale = np.float32(dh ** -0.5)
    q4 = jnp.concatenate([q_ref[:, r * dh:(r + 1) * dh] for r in range(rep)], axis=0)
    row_head = lax.broadcasted_iota(jnp.int32, (rep * tq, 1), 0) // tq
    slope4 = _head_slopes(row_head, g)
    tpos = q0 + lax.broadcasted_iota(jnp.int32, (tq, 1), 0)
    kiota = lax.broadcasted_iota(jnp.int32, (1, tk), 1)
    sel = sel_ref[...]
    blk_iota = lax.broadcasted_iota(jnp.int32, (LANES, tk), 0)
    key_iota = lax.broadcasted_iota(jnp.int32, (LANES, tk), 1)

    def attend(lo, hi, k_ref, v_ref, use_sel):
        def body(kt, carry):
            m, l, acc = carry
            k0 = pl.multiple_of(kt * tk, tk)
            k = k_ref[pl.ds(k0, tk), :]
            v = v_ref[pl.ds(k0, tk), :]
            dist = (tpos - (k0 + kiota)).astype(f32)
            if use_sel:
                expand = jnp.where(blk_iota == (k0 + key_iota) // SEL_BLOCK, 1.0, 0.0).astype(bf16)
                mask = (_dot(sel, expand) > 0.5) & (dist >= 0)
            else:
                mask = (dist >= 0) & (dist < WINDOW)
            dist4 = jnp.concatenate([dist] * rep, axis=0)
            mask4 = jnp.concatenate([mask] * rep, axis=0)
            s = _dot_nt(q4, k) * scale - slope4 * dist4
            s = jnp.where(mask4, s, NEG_BIG)
            m_new = jnp.maximum(m, jnp.max(s, axis=-1, keepdims=True))
            alpha = jnp.exp(m - m_new)
            p = jnp.where(mask4, jnp.exp(s - m_new), 0.0)
            l = alpha * l + jnp.sum(p, axis=-1, keepdims=True)
            acc = alpha * acc + _dot(p.astype(bf16), v)
            return m_new, l, acc

        init = (jnp.full((rep * tq, 1), NEG_BIG, f32), jnp.zeros((rep * tq, 1), f32),
                jnp.zeros((rep * tq, dh), f32))
        _, l, acc = lax.fori_loop(lo, hi, body, init)
        return acc / jnp.maximum(l, TINY)

    hi = (q0 + tq) // tk
    o_sel = attend(0, hi, ks_ref, vs_ref, True)
    o_win = attend(jnp.maximum(q0 - WINDOW, 0) // tk, hi, kw_ref, vw_ref, False)
    gate = _sigmoid(gt_ref[...])
    for r in range(rep):
        oc = oc_ref[:, r * dh:(r + 1) * dh].astype(f32)
        os_ = o_sel[r * tq:(r + 1) * tq]
        ow = o_win[r * tq:(r + 1) * tq]
        o = (gate[:, r:r + 1] * oc + gate[:, rep + r:rep + r + 1] * os_
             + gate[:, 2 * rep + r:2 * rep + r + 1] * ow)
        o_ref[:, r * dh:(r + 1) * dh] = o.astype(o_ref.dtype)


def nsa_selwin_attention(qkv, sel, o_cmp, gates, tq=ATT_TQ, tk=ATT_TK):
    b, t, _ = qkv.shape
    g = NSA_KV_GROUPS
    gw = NSA_REP * NSA_HEAD_DIM
    assert tq % tk == 0 and WINDOW % tk == 0 and t % tq == 0

    def kvspec(branch):
        return pl.BlockSpec((None, t, NSA_HEAD_DIM),
                            lambda i, j, k: (i, 0, NSA_HEADS + branch * g + j))

    kern = functools.partial(_selwin_kernel, tq=tq, tk=tk)
    return pl.pallas_call(
        kern,
        grid=(b, g, t // tq),
        in_specs=[pl.BlockSpec((None, tq, gw), lambda i, j, k: (i, k, j)),
                  kvspec(2), kvspec(3), kvspec(4), kvspec(5),
                  pl.BlockSpec((None, None, tq, LANES), lambda i, j, k: (i, j, k, 0)),
                  pl.BlockSpec((None, tq, gw), lambda i, j, k: (i, k, j)),
                  pl.BlockSpec((None, tq, LANES), lambda i, j, k: (i, k, j))],
        out_specs=pl.BlockSpec((None, tq, gw), lambda i, j, k: (i, k, j)),
        out_shape=jax.ShapeDtypeStruct(o_cmp.shape, bf16),
        compiler_params=_cparams("parallel", "parallel", "parallel"),
        name="nsa_selwin",
    )(qkv, qkv, qkv, qkv, qkv, sel, o_cmp, gates)


def _hgrn_tables(c):
    levels = []
    m = c // 2
    while m >= 1:
        levels.append(m)
        m //= 2
    t = np.arange(c)[:, None]
    u = np.arange(c)[None, :]
    mats = [(u <= t), (u > t)]
    masks = []
    for m in levels:
        ref = (t // (2 * m)) * 2 * m + m - 1
        right = (t % (2 * m)) >= m
        mats.append(right & (u > ref) & (u <= t))
        mats.append((~right) & (u > t) & (u <= ref))
        s = np.arange(c)[None, :]
        masks.append((t // (2 * m) == s // (2 * m)) & right & ((s % (2 * m)) < m))
    lstack = np.concatenate([x.astype(np.float32) for x in mats], axis=0)
    return levels, lstack, np.stack([x.astype(np.float32) for x in masks])


def _hgrn_kernel(q_ref, f_ref, i_ref, g_ref, la_ref, lc_ref, oml_ref, ng_ref, ls_ref, mk_ref,
                 o_ref, expo_ref, st_ref, *, c, n_levels):
    d = HGRN_DIM

    @pl.when(pl.program_id(1) == 0)
    def _():
        st_ref[...] = jnp.zeros(st_ref.shape, f32)

    z = f_ref[...]
    log_sig = jnp.minimum(z, 0.0) - jnp.log1p(jnp.exp(-jnp.abs(z)))
    y = lc_ref[...] + log_sig
    a = la_ref[...]
    log_f = jnp.maximum(a, y) + jnp.log1p(jnp.exp(-jnp.abs(a - y)))
    kk = oml_ref[...] * (1.0 / (1.0 + jnp.exp(z)))
    qf = _silu(q_ref[...])
    v = i_ref[...]
    gt = _silu(g_ref[...])
    hi, mid, lo = _split3(log_f)
    ls = ls_ref[...]
    expo_ref[...] = _dot(ls, hi) + _dot(ls, mid) + _dot(ls, lo)
    ng = ng_ref[...]
    for h in range(HGRN_HEADS):
        sl = slice(h * d, (h + 1) * d)
        b = expo_ref[0:c, sl]
        tail = expo_ref[c:2 * c, sl]
        qh = qf[:, sl]
        kh = kk[:, sl]
        vh = v[:, sl]
        vb = vh.astype(bf16)
        st = st_ref[h]
        o = _dot_nt((qh * jnp.exp(b)).astype(bf16), st.astype(bf16))
        amat = jnp.zeros((c, c), f32)
        for li in range(n_levels):
            eq = expo_ref[(2 + 2 * li) * c:(3 + 2 * li) * c, sl]
            ek = expo_ref[(3 + 2 * li) * c:(4 + 2 * li) * c, sl]
            part = _dot_nt((qh * jnp.exp(eq)).astype(bf16), (kh * jnp.exp(ek)).astype(bf16))
            amat = amat + mk_ref[li] * part
        diag = jnp.sum(qh * kh, axis=-1, keepdims=True)
        o = o + _dot(amat.astype(bf16), vb) + diag * vh
        st_ref[h] = st * jnp.exp(b[c - 1:c, :]) + _dot_tn(vb, (kh * jnp.exp(tail)).astype(bf16))
        o = o * lax.rsqrt(jnp.mean(o * o, axis=-1, keepdims=True) + EPS) * ng
        o_ref[:, sl] = (o * gt[:, sl]).astype(o_ref.dtype)


def hgrn2(proj, col0, log_lb, log1m_lb, one_m_lb, norm_g, c=HGRN_CHUNK):
    b, t, _ = proj.shape
    hd = HGRN_HEADS * HGRN_DIM
    levels, lstack, masks = _hgrn_tables(c)
    cb = col0 // hd
    assert col0 % hd == 0 and t % c == 0
    nl = len(levels)
    kern = functools.partial(_hgrn_kernel, c=c, n_levels=nl)

    def seg(k):
        return pl.BlockSpec((None, c, hd), lambda i, j: (i, j, cb + k))

    def row(width):
        return pl.BlockSpec((1, width), lambda i, j: (0, 0))

    return pl.pallas_call(
        kern,
        grid=(b, t // c),
        in_specs=[seg(0), seg(1), seg(2), seg(3), row(hd), row(hd), row(hd), row(HGRN_DIM),
                  pl.BlockSpec(lstack.shape, lambda i, j: (0, 0)),
                  pl.BlockSpec(masks.shape, lambda i, j: (0, 0, 0))],
        out_specs=pl.BlockSpec((None, c, hd), lambda i, j: (i, j, 0)),
        out_shape=jax.ShapeDtypeStruct((b, t, hd), bf16),
        scratch_shapes=[pltpu.VMEM((lstack.shape[0], hd), f32),
                        pltpu.VMEM((HGRN_HEADS, HGRN_DIM, HGRN_DIM), f32)],
        compiler_params=_cparams("parallel", "arbitrary"),
        name="hgrn2",
    )(proj, proj, proj, proj, log_lb, log1m_lb, one_m_lb, norm_g,
      jnp.asarray(lstack, bf16), jnp.asarray(masks, f32))


def _conv_kernel(x_ref, w_ref, b_ref, g_ref, bb_ref, o_ref, buf_ref, acc_ref, *, tt, sub):
    halo = CONV_HALO
    ch = x_ref.shape[-1]

    @pl.when(pl.program_id(1) == 0)
    def _():
        buf_ref[0:halo, :] = jnp.zeros((halo, ch), f32)

    buf_ref[halo:halo + tt, :] = x_ref[...]
    base = halo - (CONV_WIDTH - 1)

    def lane_block(cb, carry):
        c0 = pl.multiple_of(cb * LANES, LANES)
        w = w_ref[:, pl.ds(c0, LANES)]
        for ts in range(tt // sub):
            acc = jnp.zeros((sub, LANES), f32) + b_ref[:, pl.ds(c0, LANES)]
            for j in range(CONV_WIDTH):
                acc = acc + w[j:j + 1, :] * buf_ref[pl.ds(base + ts * sub + j, sub), pl.ds(c0, LANES)]
            acc_ref[ts * sub:(ts + 1) * sub, pl.ds(c0, LANES)] = acc
        return carry

    lax.fori_loop(0, ch // LANES, lane_block, 0)
    buf_ref[0:halo, :] = buf_ref[tt:tt + halo, :]
    y = acc_ref[...]
    mu = jnp.mean(y, axis=-1, keepdims=True)
    yc = y - mu
    var = jnp.mean(yc * yc, axis=-1, keepdims=True)
    yn = yc * lax.rsqrt(var + EPS) * g_ref[...] + bb_ref[...]
    o_ref[...] = _silu(yn).astype(o_ref.dtype)


def conv_ln_swish(x, w_dw, b_dw, ln_g, ln_b, tt=CONV_TT, sub=64):
    b, t, ch = x.shape
    tt = min(tt, t)
    assert t % tt == 0 and tt % sub == 0 and tt >= CONV_HALO
    wpad = jnp.zeros((CONV_HALO, ch), f32).at[:CONV_WIDTH].set(w_dw)
    kern = functools.partial(_conv_kernel, tt=tt, sub=sub)

    def row():
        return pl.BlockSpec((1, ch), lambda i, j: (0, 0))

    return pl.pallas_call(
        kern,
        grid=(b, t // tt),
        in_specs=[pl.BlockSpec((None, tt, ch), lambda i, j: (i, j, 0)),
                  pl.BlockSpec((CONV_HALO, ch), lambda i, j: (0, 0)), row(), row(), row()],
        out_specs=pl.BlockSpec((None, tt, ch), lambda i, j: (i, j, 0)),
        out_shape=jax.ShapeDtypeStruct((b, t, ch), bf16),
        scratch_shapes=[pltpu.VMEM((tt + CONV_HALO, ch), f32), pltpu.VMEM((tt, ch), f32)],
        compiler_params=_cparams("parallel", "arbitrary"),
        name="conv_ln_swish",
    )(x, wpad, b_dw.reshape(1, ch), ln_g.reshape(1, ch), ln_b.reshape(1, ch))


def _router_kernel(x_ref, g_ref, w_ref, tri_ref, u_ref, meta_ref, cnt_ref, run_ref):
    @pl.when(pl.program_id(0) == 0)
    def _():
        run_ref[...] = jnp.zeros(run_ref.shape, f32)

    x = x_ref[...]
    u = x * lax.rsqrt(jnp.mean(x * x, axis=-1, keepdims=True) + EPS) * g_ref[...]
    u_ref[...] = u.astype(u_ref.dtype)
    uh, um, _ = _split3(u)
    wh, wm, _ = _split3(w_ref[...])
    logits = _dot(uh, wh) + (_dot(uh, wm) + _dot(um, wh))
    lane = lax.broadcasted_iota(jnp.int32, (1, LANES), 1)
    lane_f = lane.astype(f32)
    l1 = jnp.where(lane < N_EXPERTS, logits, LOWEST)
    m1 = jnp.max(l1, axis=-1, keepdims=True)
    i1 = jnp.min(jnp.where(l1 == m1, lane_f, 1e9), axis=-1, keepdims=True)
    l2 = jnp.where(lane_f == i1, LOWEST, l1)
    m2 = jnp.max(l2, axis=-1, keepdims=True)
    i2 = jnp.min(jnp.where(l2 == m2, lane_f, 1e9), axis=-1, keepdims=True)
    e2 = jnp.exp(m2 - m1)
    g1 = 1.0 / (1.0 + e2)
    g2 = e2 / (1.0 + e2)
    hit1 = lane_f == i1
    hit2 = lane_f == i2
    onehot = jnp.where(hit1 | hit2, 1.0, 0.0)
    cum = _dot(tri_ref[...], onehot.astype(bf16)) + run_ref[...]
    r1 = jnp.sum(jnp.where(hit1, cum, 0.0), axis=-1, keepdims=True)
    r2 = jnp.sum(jnp.where(hit2, cum, 0.0), axis=-1, keepdims=True)
    run = run_ref[...] + jnp.sum(onehot, axis=0, keepdims=True)
    run_ref[...] = run
    cnt_ref[...] = run
    meta = jnp.zeros(meta_ref.shape, f32)
    for col, val in enumerate((i1, i2, g1, g2, r1, r2)):
        meta = jnp.where(lane == col, val, meta)
    meta_ref[...] = meta


def moe_router(h, norm_g, w_router, tm=ROUTER_TM):
    m, d = h.shape
    tm = min(tm, m)
    wpad = jnp.zeros((d, LANES), f32).at[:, :N_EXPERTS].set(w_router)
    tri = jnp.asarray(np.tril(np.ones((tm, tm), np.float32), -1), bf16)
    return pl.pallas_call(
        _router_kernel,
        grid=(m // tm,),
        in_specs=[pl.BlockSpec((tm, d), lambda i: (i, 0)),
                  pl.BlockSpec((1, d), lambda i: (0, 0)),
                  pl.BlockSpec((d, LANES), lambda i: (0, 0)),
                  pl.BlockSpec((tm, tm), lambda i: (0, 0))],
        out_specs=[pl.BlockSpec((tm, d), lambda i: (i, 0)),
                   pl.BlockSpec((tm, LANES), lambda i: (i, 0)),
                   pl.BlockSpec((1, LANES), lambda i: (0, 0))],
        out_shape=[jax.ShapeDtypeStruct((m, d), bf16),
                   jax.ShapeDtypeStruct((m, LANES), f32),
                   jax.ShapeDtypeStruct((1, LANES), f32)],
        scratch_shapes=[pltpu.VMEM((1, LANES), f32)],
        compiler_params=_cparams("arbitrary"),
        name="moe_router",
    )(h, norm_g.reshape(1, d), wpad, tri)


def _dispatch_kernel(dest_ref, u_ref, xs_in_ref, xs_ref, sem, *, tm):
    del xs_in_ref
    base = pl.program_id(0) * tm

    def issue(r, carry):
        for k in range(2):
            pltpu.make_async_copy(u_ref.at[base + r], xs_ref.at[dest_ref[0, 2 * r + k]], sem).start()
        return carry

    lax.fori_loop(0, tm, issue, 0)

    def drain(r, carry):
        for k in range(2):
            pltpu.make_async_copy(u_ref.at[0], xs_ref.at[0], sem).wait()
        return carry

    lax.fori_loop(0, tm, drain, 0)


def moe_dispatch(u3, dest, p_rows, tm=ROW_TILE):
    n, s, _ = u3.shape
    tm = min(tm, n)
    xs0 = jnp.zeros((p_rows, s, LANES), u3.dtype)
    kern = functools.partial(_dispatch_kernel, tm=tm)
    return pl.pallas_call(
        kern,
        grid=(n // tm,),
        in_specs=[pl.BlockSpec((None, 1, 2 * tm), lambda i: (i, 0, 0), memory_space=pltpu.SMEM),
                  pl.BlockSpec(memory_space=pl.ANY),
                  pl.BlockSpec(memory_space=pl.ANY)],
        out_specs=pl.BlockSpec(memory_space=pl.ANY),
        out_shape=jax.ShapeDtypeStruct(xs0.shape, xs0.dtype),
        scratch_shapes=[pltpu.SemaphoreType.DMA(())],
        input_output_aliases={2: 0},
        compiler_params=_cparams("arbitrary"),
        name="moe_dispatch",
    )(dest.reshape(n // tm, 1, 2 * tm), u3, xs0)


def _combine_kernel(dest_ref, gate_ref, y_ref, h_ref, o_ref, buf_ref, sem, *, tm):
    def issue(r, carry):
        for k in range(2):
            pltpu.make_async_copy(y_ref.at[dest_ref[0, 2 * r + k]], buf_ref.at[k, r], sem).start()
        return carry

    lax.fori_loop(0, tm, issue, 0)

    def drain(r, carry):
        for k in range(2):
            pltpu.make_async_copy(y_ref.at[0], buf_ref.at[0, 0], sem).wait()
        return carry

    lax.fori_loop(0, tm, drain, 0)

    def mix(r, carry):
        o_ref[r] = (h_ref[r] + gate_ref[0, 2 * r] * buf_ref[0, r]
                    + gate_ref[0, 2 * r + 1] * buf_ref[1, r])
        return carry

    lax.fori_loop(0, tm, mix, 0)


def moe_combine(y3, dest, gate, h3, tm=ROW_TILE):
    n, s, _ = h3.shape
    tm = min(tm, n)
    kern = functools.partial(_combine_kernel, tm=tm)
    return pl.pallas_call(
        kern,
        grid=(n // tm,),
        in_specs=[pl.BlockSpec((None, 1, 2 * tm), lambda i: (i, 0, 0), memory_space=pltpu.SMEM),
                  pl.BlockSpec((None, 1, 2 * tm), lambda i: (i, 0, 0), memory_space=pltpu.SMEM),
                  pl.BlockSpec(memory_space=pl.ANY),
                  pl.BlockSpec((tm, s, LANES), lambda i: (i, 0, 0))],
        out_specs=pl.BlockSpec((tm, s, LANES), lambda i: (i, 0, 0)),
        out_shape=jax.ShapeDtypeStruct(h3.shape, f32),
        scratch_shapes=[pltpu.VMEM((2, tm, s, LANES), f32), pltpu.SemaphoreType.DMA(())],
        compiler_params=_cparams("arbitrary"),
        name="moe_combine",
    )(dest.reshape(n // tm, 1, 2 * tm), gate.reshape(n // tm, 1, 2 * tm), y3, h3)


def _mixer_ab(h2, bsz, t, norm_g, w_in, w_out, pos_k, pos_v, w1_k, w1_v, w2_k, w2_v, lb, hgrn_norm):
    m, d = h2.shape
    nq = NSA_HEADS * NSA_HEAD_DIM
    nkv = NSA_KV_GROUPS * NSA_HEAD_DIM
    hd = HGRN_HEADS * HGRN_DIM
    g, rep = NSA_KV_GROUPS, NSA_REP
    u = rmsnorm(h2, norm_g, bf16)
    o_kv = nq
    o_gate = o_kv + 6 * nkv
    o_hg = o_gate + 3 * NSA_HEADS
    w_attn = w_in[:, :o_gate].astype(bf16)[None]
    w_hg = w_in[:, o_hg:o_hg + 4 * hd].astype(bf16)[None]
    wg = w_in[:, o_gate:o_hg].reshape(d, g, rep, 3).transpose(0, 1, 3, 2).reshape(d, g, 3 * rep)
    wg = jnp.pad(wg, ((0, 0), (0, 0), (0, LANES - 3 * rep))).reshape(d, g * LANES).astype(bf16)[None]

    attn = matmul(u, w_attn, out_dtype=bf16)
    hg = matmul(u, w_hg, out_dtype=f32)
    gates = matmul(u, wg, out_dtype=f32, tn=g * LANES)

    qkv = attn.reshape(bsz, t, nq + 6 * nkv)
    n_c = (t - CMP_BLOCK) // CMP_STRIDE + 1
    n_s = t // SEL_BLOCK
    rows = t // CMP_STRIDE
    xg = qkv[:, :, nq:nq + 2 * nkv].reshape(bsz, t, 2, g, NSA_HEAD_DIM).transpose(0, 2, 3, 1, 4)
    xg = xg.reshape(bsz, 2, g, rows, CMP_STRIDE * NSA_HEAD_DIM)
    pos = jnp.stack([pos_k, pos_v]).reshape(2, 1, CMP_BLOCK * NSA_HEAD_DIM)
    kvc = nsa_compress(xg, pos, jnp.stack([w1_k, w1_v]).astype(bf16),
                       jnp.stack([w2_k, w2_v]).astype(bf16), n_c)
    cj = np.arange(LANES)[:, None] * CMP_STRIDE
    sk = np.arange(LANES)[None, :] * SEL_BLOCK
    overlap = ((cj < sk + SEL_BLOCK) & (cj + CMP_BLOCK > sk)
               & (np.arange(LANES)[:, None] < n_c) & (np.arange(LANES)[None, :] < n_s))
    o_cmp, sel = nsa_cmp_attention(qkv, kvc, jnp.asarray(overlap.astype(np.float32), bf16), n_c, n_s)
    o_a = nsa_selwin_attention(qkv, sel, o_cmp, gates.reshape(bsz, t, g * LANES))

    lbc = jnp.clip(lb.astype(f32), 0.0, LB_MAX).reshape(1, hd)
    o_b = hgrn2(hg.reshape(bsz, t, 4 * hd), 0, jnp.log(jnp.maximum(lbc, TINY)), jnp.log1p(-lbc),
                1.0 - lbc, hgrn_norm.reshape(1, HGRN_DIM))
    mix = jnp.concatenate([o_a.reshape(m, nq), o_b.reshape(m, hd)], axis=-1)
    return matmul(mix, w_out.astype(bf16)[None], residual=h2, out_dtype=f32)


def _dense_ffn(h2, norm_g, w1, w3, w2):
    u = rmsnorm(h2, norm_g, bf16)
    mid = matmul_gated(u, w1.astype(bf16)[None], w3.astype(bf16)[None], act_a="silu")
    return matmul(mid, w2.astype(bf16)[None], residual=h2, out_dtype=f32, tm=512)


def _conformer(h2, bsz, t, norm_g, w_pw1, b_pw1, w_dw, b_dw, ln_g, ln_b, w_pw2, b_pw2):
    m, d = h2.shape
    ch = w_dw.shape[-1]
    u = rmsnorm(h2, norm_g, bf16)
    glu = matmul_gated(u, w_pw1[:, :ch].astype(bf16)[None], w_pw1[:, ch:].astype(bf16)[None],
                       ba=b_pw1[:ch].reshape(1, ch), bb=b_pw1[ch:].reshape(1, ch),
                       act_b="sigmoid", out_dtype=f32)
    y = conv_ln_swish(glu.reshape(bsz, t, ch), w_dw, b_dw, ln_g, ln_b)
    return matmul(y.reshape(m, ch), w_pw2.astype(bf16)[None], bias=b_pw2.reshape(1, d),
                  residual=h2, out_dtype=f32)


def _moe(h2, norm_g, w_router, w1, w3, w2, tm=MOE_TM):
    n, d = h2.shape
    e = N_EXPERTS
    s = d // LANES
    u, meta, counts = moe_router(h2, norm_g, w_router)
    tm = min(tm, n)
    idx = meta[:, 0:2].astype(jnp.int32)
    gate = meta[:, 2:4]
    rank = meta[:, 4:6].astype(jnp.int32)
    cnt = counts[0, :e].astype(jnp.int32)
    nblk = (cnt + tm - 1) // tm
    blk_end = jnp.cumsum(nblk)
    pstart = (blk_end - nblk) * tm
    dest = pstart[idx] + rank
    total_blk = (2 * n) // tm + e
    p_rows = total_blk * tm
    blk_e = jnp.minimum(jnp.searchsorted(blk_end, jnp.arange(total_blk), side="right"), e - 1)
    blk_e = blk_e.astype(jnp.int32)
    n_used = blk_end[-1:].astype(jnp.int32)

    xs = moe_dispatch(u.reshape(n, s, LANES), dest, p_rows).reshape(p_rows, d)
    mid = matmul_gated(xs, w1.astype(bf16), w3.astype(bf16), act_a="silu", tm=tm,
                       blk_e=blk_e, n_used=n_used)
    y = matmul(mid, w2.astype(bf16), out_dtype=f32, tm=tm, blk_e=blk_e, n_used=n_used)
    out = moe_combine(y.reshape(p_rows, s, LANES), dest, gate, h2.reshape(n, s, LANES))
    return out.reshape(n, d)


def _lower_bounds(table):
    p = jax.nn.softmax(table.astype(f32), axis=0)
    return jnp.cumsum(p, axis=0) - p[0]


def kernel(x, norm_mix, norm_ffn, final_norm, ab_w_in, ab_w_out, cmp_pos_k, cmp_pos_v, cmp_w1_k, cmp_w1_v, cmp_w2_k, cmp_w2_v, hgrn_lower_bounds, hgrn_norm, ffn_w1, ffn_w3, ffn_w2, conv_w_pw1, conv_b_pw1, conv_w_dw, conv_b_dw, conv_ln_g, conv_ln_b, conv_w_pw2, conv_b_pw2, moe_router, moe_w1, moe_w3, moe_w2):
    bsz, t, d = x.shape
    depth = norm_mix.shape[0]
    lbs = _lower_bounds(hgrn_lower_bounds)
    h = x.reshape(bsz * t, d)
    for layer in range(depth):
        j = layer // 2
        if layer % 2 == 0:
            h = _mixer_ab(h, bsz, t, norm_mix[layer], ab_w_in[j], ab_w_out[j], cmp_pos_k[j],
                          cmp_pos_v[j], cmp_w1_k[j], cmp_w1_v[j], cmp_w2_k[j], cmp_w2_v[j],
                          lbs[j], hgrn_norm[j])
            h = _dense_ffn(h, norm_ffn[layer], ffn_w1[j], ffn_w3[j], ffn_w2[j])
        else:
            h = _conformer(h, bsz, t, norm_mix[layer], conv_w_pw1[j], conv_b_pw1[j], conv_w_dw[j],
                           conv_b_dw[j], conv_ln_g[j], conv_ln_b[j], conv_w_pw2[j], conv_b_pw2[j])
            h = _moe(h, norm_ffn[layer], moe_router[j], moe_w1[j], moe_w3[j], moe_w2[j])
    return rmsnorm(h, final_norm, f32).reshape(bsz, t, d)
```

```python
import functools

import numpy as np
import jax
import jax.numpy as jnp
from jax import lax
from jax.experimental import pallas as pl
from jax.experimental.pallas import tpu as pltpu

f32 = jnp.float32
bf16 = jnp.bfloat16

NSA_HEADS = 8
NSA_KV_GROUPS = 2
NSA_HEAD_DIM = 128
NSA_REP = NSA_HEADS // NSA_KV_GROUPS
CMP_BLOCK = 32
CMP_STRIDE = 16
CMP_HIDDEN = 256
SEL_BLOCK = 64
SEL_TOPN = 8
WINDOW = 512
FORCE_BONUS = 1.0e4
HGRN_HEADS = 8
HGRN_DIM = 128
LB_MAX = 1.0 - 1e-6
CONV_WIDTH = 31
N_EXPERTS = 8
EPS = 1e-6
NEG_BIG = -1e30
TINY = 1e-30
LOWEST = -3.0e38
LOG2E = 1.4426950408889634

LANES = 128
SUBLANES = 8
VMEM_LIMIT = 56 * 1024 * 1024

HGRN_CHUNK = 64
ATT_TQ = 128
ATT_TK = 512
CONV_TT = 256
CONV_HALO = 32
MOE_TM = 512
ROUTER_TM = 512
ROW_TILE = 256


def _cparams(*sem):
    return pltpu.CompilerParams(dimension_semantics=sem, vmem_limit_bytes=VMEM_LIMIT)


def _largest_tile(n, cap, unit):
    if n <= cap:
        return n
    best = max(k for k in range(unit, cap + 1, unit) if n % k == 0)
    return best


def _dot(a, b):
    return jnp.dot(a, b, preferred_element_type=f32)


def _dot_nt(a, b):
    return lax.dot_general(a, b, (((1,), (1,)), ((), ())), preferred_element_type=f32)


def _dot_tn(a, b):
    return lax.dot_general(a, b, (((0,), (0,)), ((), ())), preferred_element_type=f32)


def _sigmoid(x):
    return 1.0 / (1.0 + jnp.exp(-x))


def _silu(x):
    return x * _sigmoid(x)


def _split3(x):
    hi = x.astype(bf16)
    r = x - hi.astype(f32)
    mid = r.astype(bf16)
    lo = (r - mid.astype(f32)).astype(bf16)
    return hi, mid, lo


def _rmsnorm_kernel(x_ref, g_ref, o_ref):
    x = x_ref[...]
    y = x * lax.rsqrt(jnp.mean(x * x, axis=-1, keepdims=True) + EPS) * g_ref[...]
    o_ref[...] = y.astype(o_ref.dtype)


def rmsnorm(x, g, out_dtype, tm=ROW_TILE):
    m, d = x.shape
    return pl.pallas_call(
        _rmsnorm_kernel,
        grid=(m // tm,),
        in_specs=[pl.BlockSpec((tm, d), lambda i: (i, 0)),
                  pl.BlockSpec((1, d), lambda i: (0, 0))],
        out_specs=pl.BlockSpec((tm, d), lambda i: (i, 0)),
        out_shape=jax.ShapeDtypeStruct((m, d), out_dtype),
        compiler_params=_cparams("parallel"),
        name="rmsnorm",
    )(x, g.reshape(1, d))


def _mm_kernel(be_ref, nu_ref, x_ref, w_ref, *rest, has_bias, has_res):
    rest = list(rest)
    b_ref = rest.pop(0) if has_bias else None
    r_ref = rest.pop(0) if has_res else None
    o_ref = rest.pop(0)
    i = pl.program_id(0)

    @pl.when(i < nu_ref[0])
    def _():
        acc = _dot(x_ref[...], w_ref[...])
        if has_bias:
            acc = acc + b_ref[...]
        if has_res:
            acc = acc + r_ref[...]
        o_ref[...] = acc.astype(o_ref.dtype)

    @pl.when(i >= nu_ref[0])
    def _():
        o_ref[...] = jnp.zeros(o_ref.shape, o_ref.dtype)


def matmul(x, w, *, bias=None, residual=None, out_dtype=f32, tm=1024, tn=512,
           blk_e=None, n_used=None):
    m, k = x.shape
    n = w.shape[-1]
    tm = _largest_tile(m, tm, SUBLANES)
    tn = _largest_tile(n, tn, LANES)
    assert m % tm == 0 and n % tn == 0, (m, n, tm, tn)
    nblk = m // tm
    if blk_e is None:
        blk_e = jnp.zeros((nblk,), jnp.int32)
        n_used = jnp.full((1,), nblk, jnp.int32)

    def jj(i, j, nu):
        return jnp.where(i < nu[0], j, 0)

    in_specs = [pl.BlockSpec((tm, k), lambda i, j, be, nu: (i, 0)),
                pl.BlockSpec((None, k, tn), lambda i, j, be, nu: (be[i], 0, jj(i, j, nu)))]
    args = [x, w]
    if bias is not None:
        in_specs.append(pl.BlockSpec((1, tn), lambda i, j, be, nu: (0, j)))
        args.append(bias)
    if residual is not None:
        in_specs.append(pl.BlockSpec((tm, tn), lambda i, j, be, nu: (i, j)))
        args.append(residual)
    kern = functools.partial(_mm_kernel, has_bias=bias is not None, has_res=residual is not None)
    return pl.pallas_call(
        kern,
        grid_spec=pltpu.PrefetchScalarGridSpec(
            num_scalar_prefetch=2, grid=(nblk, n // tn), in_specs=in_specs,
            out_specs=pl.BlockSpec((tm, tn), lambda i, j, be, nu: (i, j))),
        out_shape=jax.ShapeDtypeStruct((m, n), out_dtype),
        compiler_params=_cparams("parallel", "arbitrary"),
        name="matmul",
    )(blk_e, n_used, *args)


def _mm2_kernel(be_ref, nu_ref, x_ref, wa_ref, wb_ref, *rest, has_bias, act_a, act_b):
    rest = list(rest)
    ba_ref = rest.pop(0) if has_bias else None
    bb_ref = rest.pop(0) if has_bias else None
    o_ref = rest.pop(0)
    i = pl.program_id(0)

    @pl.when(i < nu_ref[0])
    def _():
        x = x_ref[...]
        a = _dot(x, wa_ref[...])
        b = _dot(x, wb_ref[...])
        if has_bias:
            a = a + ba_ref[...]
            b = b + bb_ref[...]
        if act_a == "silu":
            a = _silu(a)
        if act_b == "sigmoid":
            b = _sigmoid(b)
        o_ref[...] = (a * b).astype(o_ref.dtype)

    @pl.when(i >= nu_ref[0])
    def _():
        o_ref[...] = jnp.zeros(o_ref.shape, o_ref.dtype)


def matmul_gated(x, wa, wb, *, ba=None, bb=None, act_a=None, act_b=None, out_dtype=bf16,
                 tm=1024, tn=512, blk_e=None, n_used=None):
    m, k = x.shape
    n = wa.shape[-1]
    tm = _largest_tile(m, tm, SUBLANES)
    tn = _largest_tile(n, tn, LANES)
    assert m % tm == 0 and n % tn == 0, (m, n, tm, tn)
    nblk = m // tm
    if blk_e is None:
        blk_e = jnp.zeros((nblk,), jnp.int32)
        n_used = jnp.full((1,), nblk, jnp.int32)

    def wmap(i, j, be, nu):
        return (be[i], 0, jnp.where(i < nu[0], j, 0))

    in_specs = [pl.BlockSpec((tm, k), lambda i, j, be, nu: (i, 0)),
                pl.BlockSpec((None, k, tn), wmap),
                pl.BlockSpec((None, k, tn), wmap)]
    args = [x, wa, wb]
    if ba is not None:
        in_specs += [pl.BlockSpec((1, tn), lambda i, j, be, nu: (0, j))] * 2
        args += [ba, bb]
    kern = functools.partial(_mm2_kernel, has_bias=ba is not None, act_a=act_a, act_b=act_b)
    return pl.pallas_call(
        kern,
        grid_spec=pltpu.PrefetchScalarGridSpec(
            num_scalar_prefetch=2, grid=(nblk, n // tn), in_specs=in_specs,
            out_specs=pl.BlockSpec((tm, tn), lambda i, j, be, nu: (i, j))),
        out_shape=jax.ShapeDtypeStruct((m, n), out_dtype),
        compiler_params=_cparams("parallel", "arbitrary"),
        name="matmul_gated",
    )(blk_e, n_used, *args)


def _gelu_tanh(x):
    c = np.sqrt(2.0 / np.pi).astype(np.float32)
    return 0.5 * x * (1.0 + jnp.tanh(c * (x + 0.044715 * (x * x * x))))


def _compress_kernel(x_ref, pos_ref, w1_ref, w2_ref, o_ref, *, n_c):
    half = CMP_STRIDE * NSA_HEAD_DIM
    x = x_ref[...].astype(f32)
    rows = x.shape[0]
    xn = pltpu.roll(x, rows - 1, 0)
    pos = pos_ref[...]
    xa = (x + pos[:, :half]).astype(bf16)
    xb = (xn + pos[:, half:]).astype(bf16)
    h = _dot(xa, w1_ref[:half, :]) + _dot(xb, w1_ref[half:, :])
    o = _dot(_gelu_tanh(h).astype(bf16), w2_ref[...])
    ridx = lax.broadcasted_iota(jnp.int32, o.shape, 0)
    o_ref[...] = jnp.where(ridx < n_c, o, 0.0).astype(o_ref.dtype)


def nsa_compress(xg, pos, w1, w2, n_c):
    b, _, g, rows, width = xg.shape
    kern = functools.partial(_compress_kernel, n_c=n_c)
    return pl.pallas_call(
        kern,
        grid=(b, 2, g),
        in_specs=[pl.BlockSpec((None, None, None, rows, width), lambda i, s, j: (i, s, j, 0, 0)),
                  pl.BlockSpec((None, 1, 2 * width), lambda i, s, j: (s, 0, 0)),
                  pl.BlockSpec((None, 2 * width, CMP_HIDDEN), lambda i, s, j: (s, 0, 0)),
                  pl.BlockSpec((None, CMP_HIDDEN, NSA_HEAD_DIM), lambda i, s, j: (s, 0, 0))],
        out_specs=pl.BlockSpec((None, None, None, rows, NSA_HEAD_DIM),
                               lambda i, s, j: (i, s, j, 0, 0)),
        out_shape=jax.ShapeDtypeStruct((b, 2, g, rows, NSA_HEAD_DIM), bf16),
        compiler_params=_cparams("parallel", "parallel", "parallel"),
        name="nsa_compress",
    )(xg, pos, w1, w2)


def _head_slopes(rows_head, g):
    h = g * NSA_REP + rows_head
    out = jnp.zeros(h.shape, f32)
    for hh in range(NSA_HEADS):
        out = jnp.where(h == hh, np.float32(2.0 ** (-8.0 * (hh + 1) / NSA_HEADS)), out)
    return out


def _cmp_kernel(q_ref, kc_ref, vc_ref, ov_ref, o_ref, selt_ref, *, n_c, n_s, tq):
    g = pl.program_id(1)
    q0 = pl.program_id(2) * tq
    scale = np.float32(NSA_HEAD_DIM ** -0.5)
    tpos = q0 + lax.broadcasted_iota(jnp.int32, (tq, 1), 0)
    lane = lax.broadcasted_iota(jnp.int32, (1, LANES), 1)
    c_dist = (tpos - (lane * CMP_STRIDE + (CMP_BLOCK - 1))).astype(f32)
    cmask = (c_dist >= 0) & (lane < n_c)
    kc = kc_ref[...]
    vc = vc_ref[...]
    psum = jnp.zeros((tq, LANES), f32)
    for r in range(NSA_REP):
        slope = _head_slopes(jnp.full((1, 1), r, jnp.int32), g)
        qr = q_ref[:, r * NSA_HEAD_DIM:(r + 1) * NSA_HEAD_DIM]
        s = _dot_nt(qr, kc) * scale - slope * c_dist
        s = jnp.where(cmask, s, NEG_BIG)
        m = jnp.max(s, axis=-1, keepdims=True)
        e = jnp.where(cmask, jnp.exp(s - m), 0.0)
        p = e / jnp.maximum(jnp.sum(e, axis=-1, keepdims=True), TINY)
        o_ref[:, r * NSA_HEAD_DIM:(r + 1) * NSA_HEAD_DIM] = _dot(p.astype(bf16), vc).astype(o_ref.dtype)
        psum = psum + p
    hi, mid, lo = _split3(psum)
    ov = ov_ref[...]
    imp = _dot(hi, ov) + _dot(mid, ov) + _dot(lo, ov)
    cur = tpos // SEL_BLOCK
    valid = lane * SEL_BLOCK <= tpos
    forced = valid & ((lane == 0) | (lane == cur) | (lane == cur - 1))
    score = jnp.where(valid, imp, -1.0) + jnp.where(forced, FORCE_BONUS, 0.0)
    score = jnp.where(lane < n_s, score, LOWEST)
    lane_f = lane.astype(f32)
    sel = jnp.zeros((tq, LANES), f32)
    for _ in range(min(SEL_TOPN, n_s)):
        m = jnp.max(score, axis=-1, keepdims=True)
        first = jnp.min(jnp.where(score == m, lane_f, 1e9), axis=-1, keepdims=True)
        hit = lane_f == first
        sel = jnp.where(hit, 1.0, sel)
        score = jnp.where(hit, LOWEST, score)
    selt_ref[...] = sel.T[:selt_ref.shape[0], :]


def nsa_cmp_attention(q, kvc, overlap, n_c, n_s, tq=ROW_TILE):
    b, t, _ = q.shape
    g = NSA_KV_GROUPS
    ncp = kvc.shape[3]
    assert ncp == LANES, "compressed blocks are laid out on one lane tile"
    assert n_s % SUBLANES == 0 and n_s <= LANES
    gw = NSA_REP * NSA_HEAD_DIM
    kern = functools.partial(_cmp_kernel, n_c=n_c, n_s=n_s, tq=tq)
    return pl.pallas_call(
        kern,
        grid=(b, g, t // tq),
        in_specs=[pl.BlockSpec((None, tq, gw), lambda i, j, k: (i, k, j)),
                  pl.BlockSpec((None, None, None, ncp, NSA_HEAD_DIM), lambda i, j, k: (i, 0, j, 0, 0)),
                  pl.BlockSpec((None, None, None, ncp, NSA_HEAD_DIM), lambda i, j, k: (i, 1, j, 0, 0)),
                  pl.BlockSpec((LANES, LANES), lambda i, j, k: (0, 0))],
        out_specs=[pl.BlockSpec((None, tq, gw), lambda i, j, k: (i, k, j)),
                   pl.BlockSpec((None, None, n_s, tq), lambda i, j, k: (i, j, 0, k))],
        out_shape=[jax.ShapeDtypeStruct((b, t, g * gw), bf16),
                   jax.ShapeDtypeStruct((b, g, n_s, t), f32)],
        compiler_params=_cparams("parallel", "parallel", "parallel"),
        name="nsa_cmp",
    )(q, kvc, kvc, overlap)


def _selwin_kernel(q_ref, ks_ref, vs_ref, kw_ref, vw_ref, selt_ref, oc_ref, gt_ref, o_ref, *, tq, tk):
    g = pl.program_id(1)
    q0 = pl.program_id(2) * tq
    rep = NSA_REP
    dh = NSA_HEAD_DIM
    nq = rep * tq
    span = WINDOW + tq
    c_qk = LOG2E * dh ** -0.5
    q4 = jnp.concatenate([q_ref[:, r * dh:(r + 1) * dh] for r in range(rep)], axis=0)
    nslope = [-LOG2E * _head_slopes(jnp.full((1, 1), r, jnp.int32), g) for r in range(rep)]

    def scores(k, dist, ok):
        bias = jnp.concatenate([jnp.where(ok, nslope[r] * dist, NEG_BIG) for r in range(rep)], axis=1)
        return _dot_nt(k, q4) * c_qk + bias

    def rel_pos(rows):
        return (q0 + lax.broadcasted_iota(jnp.int32, (rows, tq), 1)
                - lax.broadcasted_iota(jnp.int32, (rows, tq), 0))

    d_sel = rel_pos(tk)
    per_tile = tk // SEL_BLOCK

    def sel_body(kt, carry):
        m, l, acc = carry
        k0 = pl.multiple_of(kt * tk, tk)
        dist = (d_sel - k0).astype(f32)
        flags = selt_ref[pl.ds(pl.multiple_of(kt * per_tile, per_tile), per_tile), :]
        picked = jnp.concatenate([jnp.broadcast_to(flags[i:i + 1, :], (SEL_BLOCK, tq))
                                  for i in range(per_tile)], axis=0)
        ok = jnp.where(dist >= 0, picked, 0.0) > 0.5
        s = scores(ks_ref[pl.ds(k0, tk), :], dist, ok)
        m_new = jnp.maximum(m, jnp.max(s, axis=0, keepdims=True))
        alpha = jnp.exp2(m - m_new)
        p = jnp.exp2(s - m_new)
        l = alpha * l + jnp.sum(p, axis=0, keepdims=True)
        acc = alpha * acc + _dot_tn(vs_ref[pl.ds(k0, tk), :], p.astype(bf16))
        return m_new, l, acc

    init = (jnp.full((1, nq), NEG_BIG, f32), jnp.zeros((1, nq), f32), jnp.zeros((dh, nq), f32))
    _, l, acc = lax.fori_loop(0, (q0 + tq + tk - 1) // tk, sel_body, init)
    ot_sel = acc / jnp.maximum(l, TINY)

    k_lo = pl.multiple_of(jnp.maximum(q0 - WINDOW, 0), tq)
    dist = (rel_pos(span) - k_lo).astype(f32)
    ok = jnp.abs(dist - 0.5 * (WINDOW - 1)) <= 0.5 * (WINDOW - 1)
    s = scores(kw_ref[pl.ds(k_lo, span), :], dist, ok)
    m = jnp.max(s, axis=0, keepdims=True)
    p = jnp.exp2(s - m)
    l = jnp.sum(p, axis=0, keepdims=True)
    ot_win = _dot_tn(vw_ref[pl.ds(k_lo, span), :], p.astype(bf16)) / jnp.maximum(l, TINY)

    gate = _sigmoid(gt_ref[...])
    for r in range(rep):
        oc = oc_ref[:, r * dh:(r + 1) * dh].astype(f32)
        os_ = ot_sel[:, r * tq:(r + 1) * tq].T
        ow = ot_win[:, r * tq:(r + 1) * tq].T
        o = (gate[:, r:r + 1] * oc + gate[:, rep + r:rep + r + 1] * os_
             + gate[:, 2 * rep + r:2 * rep + r + 1] * ow)
        o_ref[:, r * dh:(r + 1) * dh] = o.astype(o_ref.dtype)


def nsa_selwin_attention(qkv, sel, o_cmp, gates, tq=ATT_TQ, tk=ATT_TK):
    b, t, _ = qkv.shape
    g = NSA_KV_GROUPS
    gw = NSA_REP * NSA_HEAD_DIM
    n_s = sel.shape[2]
    assert t % tk == 0 and t % tq == 0 and WINDOW % tq == 0 and t >= WINDOW + tq
    assert tk % (SEL_BLOCK * SUBLANES) == 0 and n_s * SEL_BLOCK == t

    def kvspec(branch):
        return pl.BlockSpec((None, t, NSA_HEAD_DIM),
                            lambda i, j, k: (i, 0, NSA_HEADS + branch * g + j))

    kern = functools.partial(_selwin_kernel, tq=tq, tk=tk)
    return pl.pallas_call(
        kern,
        grid=(b, g, t // tq),
        in_specs=[pl.BlockSpec((None, tq, gw), lambda i, j, k: (i, k, j)),
                  kvspec(2), kvspec(3), kvspec(4), kvspec(5),
                  pl.BlockSpec((None, None, n_s, tq), lambda i, j, k: (i, j, 0, k)),
                  pl.BlockSpec((None, tq, gw), lambda i, j, k: (i, k, j)),
                  pl.BlockSpec((None, tq, LANES), lambda i, j, k: (i, k, j))],
        out_specs=pl.BlockSpec((None, tq, gw), lambda i, j, k: (i, k, j)),
        out_shape=jax.ShapeDtypeStruct(o_cmp.shape, bf16),
        compiler_params=_cparams("parallel", "parallel", "parallel"),
        name="nsa_selwin",
    )(qkv, qkv, qkv, qkv, qkv, sel, o_cmp, gates)


def _hgrn_tables(c):
    levels = []
    m = c // 2
    while m >= 1:
        levels.append(m)
        m //= 2
    t = np.arange(c)[:, None]
    u = np.arange(c)[None, :]
    mats = [(u <= t), (u > t)]
    masks = []
    for m in levels:
        ref = (t // (2 * m)) * 2 * m + m - 1
        right = (t % (2 * m)) >= m
        mats.append(u <= ref)
        s = np.arange(c)[None, :]
        masks.append((t // (2 * m) == s // (2 * m)) & right & ((s % (2 * m)) < m))
    lstack = np.concatenate([x.astype(np.float32) for x in mats], axis=0)
    return levels, lstack, np.stack([x.astype(np.float32) for x in masks])


def _hgrn_kernel(q_ref, f_ref, i_ref, g_ref, la_ref, lc_ref, oml_ref, ng_ref, ls_ref, mk_ref,
                 o_ref, expo_ref, st_ref, *, c, n_levels):
    d = HGRN_DIM

    @pl.when(pl.program_id(1) == 0)
    def _():
        st_ref[...] = jnp.zeros(st_ref.shape, f32)

    z = f_ref[...]
    log_sig = jnp.minimum(z, 0.0) - jnp.log(1.0 + jnp.exp(-jnp.abs(z)))
    y = lc_ref[...] + log_sig
    a = la_ref[...]
    log_f = jnp.maximum(a, y) + jnp.log(1.0 + jnp.exp(-jnp.abs(a - y)))
    kk = oml_ref[...] * (1.0 / (1.0 + jnp.exp(z)))
    qf = _silu(q_ref[...])
    v = i_ref[...]
    gt = _silu(g_ref[...])
    hi, mid, lo = _split3(log_f)
    ls = ls_ref[...]
    expo_ref[...] = _dot(ls, hi) + _dot(ls, mid) + _dot(ls, lo)
    ng = ng_ref[...]
    for h in range(HGRN_HEADS):
        sl = slice(h * d, (h + 1) * d)
        b = expo_ref[0:c, sl]
        tail = expo_ref[c:2 * c, sl]
        qh = qf[:, sl]
        kh = kk[:, sl]
        vh = v[:, sl]
        vb = vh.astype(bf16)
        st = st_ref[h]
        o = _dot_nt((qh * jnp.exp(b)).astype(bf16), st.astype(bf16))
        amat = jnp.zeros((c, c), f32)
        for li in range(n_levels):
            e = jnp.exp(-jnp.abs(b - expo_ref[(2 + li) * c:(3 + li) * c, sl]))
            part = _dot_nt((qh * e).astype(bf16), (kh * e).astype(bf16))
            amat = amat + mk_ref[li] * part
        diag = jnp.sum(qh * kh, axis=-1, keepdims=True)
        o = o + _dot(amat.astype(bf16), vb) + diag * vh
        st_ref[h] = st * jnp.exp(b[c - 1:c, :]) + _dot_tn(vb, (kh * jnp.exp(tail)).astype(bf16))
        o = o * lax.rsqrt(jnp.mean(o * o, axis=-1, keepdims=True) + EPS) * ng
        o_ref[:, sl] = (o * gt[:, sl]).astype(o_ref.dtype)


def hgrn2(proj, col0, log_lb, log1m_lb, one_m_lb, norm_g, c=HGRN_CHUNK):
    b, t, _ = proj.shape
    hd = HGRN_HEADS * HGRN_DIM
    levels, lstack, masks = _hgrn_tables(c)
    cb = col0 // hd
    assert col0 % hd == 0 and t % c == 0
    nl = len(levels)
    kern = functools.partial(_hgrn_kernel, c=c, n_levels=nl)

    def seg(k):
        return pl.BlockSpec((None, c, hd), lambda i, j: (i, j, cb + k))

    def row(width):
        return pl.BlockSpec((1, width), lambda i, j: (0, 0))

    return pl.pallas_call(
        kern,
        grid=(b, t // c),
        in_specs=[seg(0), seg(1), seg(2), seg(3), row(hd), row(hd), row(hd), row(HGRN_DIM),
                  pl.BlockSpec(lstack.shape, lambda i, j: (0, 0)),
                  pl.BlockSpec(masks.shape, lambda i, j: (0, 0, 0))],
        out_specs=pl.BlockSpec((None, c, hd), lambda i, j: (i, j, 0)),
        out_shape=jax.ShapeDtypeStruct((b, t, hd), bf16),
        scratch_shapes=[pltpu.VMEM((lstack.shape[0], hd), f32),
                        pltpu.VMEM((HGRN_HEADS, HGRN_DIM, HGRN_DIM), f32)],
        compiler_params=_cparams("parallel", "arbitrary"),
        name="hgrn2",
    )(proj, proj, proj, proj, log_lb, log1m_lb, one_m_lb, norm_g,
      jnp.asarray(lstack, bf16), jnp.asarray(masks, f32))


def _conv_kernel(x_ref, w_ref, b_ref, g_ref, bb_ref, o_ref, buf_ref, acc_ref, *, tt, sub):
    halo = CONV_HALO
    ch = x_ref.shape[-1]

    @pl.when(pl.program_id(1) == 0)
    def _():
        buf_ref[0:halo, :] = jnp.zeros((halo, ch), f32)

    buf_ref[halo:halo + tt, :] = x_ref[...]
    base = halo - (CONV_WIDTH - 1)

    def lane_block(cb, carry):
        c0 = pl.multiple_of(cb * LANES, LANES)
        w = w_ref[:, pl.ds(c0, LANES)]
        for ts in range(tt // sub):
            acc = jnp.zeros((sub, LANES), f32) + b_ref[:, pl.ds(c0, LANES)]
            for phase in range(SUBLANES):
                taps = [j for j in range(CONV_WIDTH) if (base + j) % SUBLANES == phase]
                reach = max((base + j) // SUBLANES for j in taps) * SUBLANES
                if phase == 0:
                    slab = buf_ref[pl.ds(ts * sub, sub + reach), pl.ds(c0, LANES)]
                else:
                    rows = sub + reach + SUBLANES
                    slab = pltpu.roll(buf_ref[pl.ds(ts * sub, rows), pl.ds(c0, LANES)], rows - phase, 0)
                for j in taps:
                    off = (base + j) // SUBLANES * SUBLANES
                    acc = acc + w[j:j + 1, :] * slab[off:off + sub, :]
            acc_ref[ts * sub:(ts + 1) * sub, pl.ds(c0, LANES)] = acc
        return carry

    lax.fori_loop(0, ch // LANES, lane_block, 0)
    buf_ref[0:halo, :] = buf_ref[tt:tt + halo, :]
    y = acc_ref[...]
    mu = jnp.mean(y, axis=-1, keepdims=True)
    yc = y - mu
    var = jnp.mean(yc * yc, axis=-1, keepdims=True)
    yn = yc * lax.rsqrt(var + EPS) * g_ref[...] + bb_ref[...]
    o_ref[...] = _silu(yn).astype(o_ref.dtype)


def conv_ln_swish(x, w_dw, b_dw, ln_g, ln_b, tt=CONV_TT, sub=64):
    b, t, ch = x.shape
    tt = min(tt, t)
    assert t % tt == 0 and tt % sub == 0 and tt >= CONV_HALO
    wpad = jnp.zeros((CONV_HALO, ch), f32).at[:CONV_WIDTH].set(w_dw)
    kern = functools.partial(_conv_kernel, tt=tt, sub=sub)

    def row():
        return pl.BlockSpec((1, ch), lambda i, j: (0, 0))

    return pl.pallas_call(
        kern,
        grid=(b, t // tt),
        in_specs=[pl.BlockSpec((None, tt, ch), lambda i, j: (i, j, 0)),
                  pl.BlockSpec((CONV_HALO, ch), lambda i, j: (0, 0)), row(), row(), row()],
        out_specs=pl.BlockSpec((None, tt, ch), lambda i, j: (i, j, 0)),
        out_shape=jax.ShapeDtypeStruct((b, t, ch), bf16),
        scratch_shapes=[pltpu.VMEM((tt + CONV_HALO, ch), f32), pltpu.VMEM((tt, ch), f32)],
        compiler_params=_cparams("parallel", "arbitrary"),
        name="conv_ln_swish",
    )(x, wpad, b_dw.reshape(1, ch), ln_g.reshape(1, ch), ln_b.reshape(1, ch))


def _router_kernel(x_ref, g_ref, w_ref, tri_ref, u_ref, meta_ref, cnt_ref, run_ref):
    @pl.when(pl.program_id(0) == 0)
    def _():
        run_ref[...] = jnp.zeros(run_ref.shape, f32)

    x = x_ref[...]
    u = x * lax.rsqrt(jnp.mean(x * x, axis=-1, keepdims=True) + EPS) * g_ref[...]
    u_ref[...] = u.astype(u_ref.dtype)
    uh, um, _ = _split3(u)
    wh, wm, _ = _split3(w_ref[...])
    logits = _dot(uh, wh) + (_dot(uh, wm) + _dot(um, wh))
    lane = lax.broadcasted_iota(jnp.int32, (1, LANES), 1)
    lane_f = lane.astype(f32)
    l1 = jnp.where(lane < N_EXPERTS, logits, LOWEST)
    m1 = jnp.max(l1, axis=-1, keepdims=True)
    i1 = jnp.min(jnp.where(l1 == m1, lane_f, 1e9), axis=-1, keepdims=True)
    l2 = jnp.where(lane_f == i1, LOWEST, l1)
    m2 = jnp.max(l2, axis=-1, keepdims=True)
    i2 = jnp.min(jnp.where(l2 == m2, lane_f, 1e9), axis=-1, keepdims=True)
    e2 = jnp.exp(m2 - m1)
    g1 = 1.0 / (1.0 + e2)
    g2 = e2 / (1.0 + e2)
    hit1 = lane_f == i1
    hit2 = lane_f == i2
    onehot = jnp.where(hit1 | hit2, 1.0, 0.0)
    cum = _dot(tri_ref[...], onehot.astype(bf16)) + run_ref[...]
    r1 = jnp.sum(jnp.where(hit1, cum, 0.0), axis=-1, keepdims=True)
    r2 = jnp.sum(jnp.where(hit2, cum, 0.0), axis=-1, keepdims=True)
    run = run_ref[...] + jnp.sum(onehot, axis=0, keepdims=True)
    run_ref[...] = run
    cnt_ref[...] = run
    meta = jnp.zeros(meta_ref.shape, f32)
    for col, val in enumerate((i1, i2, g1, g2, r1, r2)):
        meta = jnp.where(lane == col, val, meta)
    meta_ref[...] = meta


def moe_router(h, norm_g, w_router, tm=ROUTER_TM):
    m, d = h.shape
    tm = min(tm, m)
    wpad = jnp.zeros((d, LANES), f32).at[:, :N_EXPERTS].set(w_router)
    tri = jnp.asarray(np.tril(np.ones((tm, tm), np.float32), -1), bf16)
    return pl.pallas_call(
        _router_kernel,
        grid=(m // tm,),
        in_specs=[pl.BlockSpec((tm, d), lambda i: (i, 0)),
                  pl.BlockSpec((1, d), lambda i: (0, 0)),
                  pl.BlockSpec((d, LANES), lambda i: (0, 0)),
                  pl.BlockSpec((tm, tm), lambda i: (0, 0))],
        out_specs=[pl.BlockSpec((tm, d), lambda i: (i, 0)),
                   pl.BlockSpec((tm, LANES), lambda i: (i, 0)),
                   pl.BlockSpec((1, LANES), lambda i: (0, 0))],
        out_shape=[jax.ShapeDtypeStruct((m, d), bf16),
                   jax.ShapeDtypeStruct((m, LANES), f32),
                   jax.ShapeDtypeStruct((1, LANES), f32)],
        scratch_shapes=[pltpu.VMEM((1, LANES), f32)],
        compiler_params=_cparams("arbitrary"),
        name="moe_router",
    )(h, norm_g.reshape(1, d), wpad, tri)


def _dispatch_kernel(dest_ref, u_ref, xs_in_ref, xs_ref, sem, *, tm):
    del xs_in_ref

    def issue(r, carry):
        for k in range(2):
            pltpu.make_async_copy(u_ref.at[r], xs_ref.at[dest_ref[0, 2 * r + k]], sem).start()
        return carry

    lax.fori_loop(0, tm, issue, 0)

    def drain(r, carry):
        for k in range(2):
            pltpu.make_async_copy(u_ref.at[0], xs_ref.at[0], sem).wait()
        return carry

    lax.fori_loop(0, tm, drain, 0)


def moe_dispatch(u3, dest, p_rows, tm=ROW_TILE):
    n, s, _ = u3.shape
    tm = min(tm, n)
    xs0 = jnp.zeros((p_rows, s, LANES), u3.dtype)
    kern = functools.partial(_dispatch_kernel, tm=tm)
    return pl.pallas_call(
        kern,
        grid=(n // tm,),
        in_specs=[pl.BlockSpec((None, 1, 2 * tm), lambda i: (i, 0, 0), memory_space=pltpu.SMEM),
                  pl.BlockSpec((tm, s, LANES), lambda i: (i, 0, 0)),
                  pl.BlockSpec(memory_space=pl.ANY)],
        out_specs=pl.BlockSpec(memory_space=pl.ANY),
        out_shape=jax.ShapeDtypeStruct(xs0.shape, xs0.dtype),
        scratch_shapes=[pltpu.SemaphoreType.DMA(())],
        input_output_aliases={2: 0},
        compiler_params=_cparams("arbitrary"),
        name="moe_dispatch",
    )(dest.reshape(n // tm, 1, 2 * tm), u3, xs0)


def _combine_kernel(dest_ref, gate_ref, y_ref, h_ref, o_ref, buf_ref, sem, *, tm):
    def issue(r, carry):
        for k in range(2):
            pltpu.make_async_copy(y_ref.at[dest_ref[0, 2 * r + k]], buf_ref.at[k, r], sem).start()
        return carry

    lax.fori_loop(0, tm, issue, 0)

    def drain(r, carry):
        for k in range(2):
            pltpu.make_async_copy(y_ref.at[0], buf_ref.at[0, 0], sem).wait()
        return carry

    lax.fori_loop(0, tm, drain, 0)

    def mix(r, carry):
        o_ref[r] = (h_ref[r] + gate_ref[0, 2 * r] * buf_ref[0, r]
                    + gate_ref[0, 2 * r + 1] * buf_ref[1, r])
        return carry

    lax.fori_loop(0, tm, mix, 0)


def moe_combine(y3, dest, gate, h3, tm=ROW_TILE):
    n, s, _ = h3.shape
    tm = min(tm, n)
    kern = functools.partial(_combine_kernel, tm=tm)
    return pl.pallas_call(
        kern,
        grid=(n // tm,),
        in_specs=[pl.BlockSpec((None, 1, 2 * tm), lambda i: (i, 0, 0), memory_space=pltpu.SMEM),
                  pl.BlockSpec((None, 1, 2 * tm), lambda i: (i, 0, 0), memory_space=pltpu.SMEM),
                  pl.BlockSpec(memory_space=pl.ANY),
                  pl.BlockSpec((tm, s, LANES), lambda i: (i, 0, 0))],
        out_specs=pl.BlockSpec((tm, s, LANES), lambda i: (i, 0, 0)),
        out_shape=jax.ShapeDtypeStruct(h3.shape, f32),
        scratch_shapes=[pltpu.VMEM((2, tm, s, LANES), f32), pltpu.SemaphoreType.DMA(())],
        compiler_params=_cparams("arbitrary"),
        name="moe_combine",
    )(dest.reshape(n // tm, 1, 2 * tm), gate.reshape(n // tm, 1, 2 * tm), y3, h3)


def _mixer_ab(h2, bsz, t, norm_g, w_in, w_out, pos_k, pos_v, w1_k, w1_v, w2_k, w2_v, lb, hgrn_norm):
    m, d = h2.shape
    nq = NSA_HEADS * NSA_HEAD_DIM
    nkv = NSA_KV_GROUPS * NSA_HEAD_DIM
    hd = HGRN_HEADS * HGRN_DIM
    g, rep = NSA_KV_GROUPS, NSA_REP
    u = rmsnorm(h2, norm_g, bf16)
    o_kv = nq
    o_gate = o_kv + 6 * nkv
    o_hg = o_gate + 3 * NSA_HEADS
    w_attn = w_in[:, :o_gate].astype(bf16)[None]
    w_hg = w_in[:, o_hg:o_hg + 4 * hd].astype(bf16)[None]
    wg = w_in[:, o_gate:o_hg].reshape(d, g, rep, 3).transpose(0, 1, 3, 2).reshape(d, g, 3 * rep)
    wg = jnp.pad(wg, ((0, 0), (0, 0), (0, LANES - 3 * rep))).reshape(d, g * LANES).astype(bf16)[None]

    attn = matmul(u, w_attn, out_dtype=bf16)
    hg = matmul(u, w_hg, out_dtype=f32, tn=1024)
    gates = matmul(u, wg, out_dtype=f32, tn=g * LANES)

    qkv = attn.reshape(bsz, t, nq + 6 * nkv)
    n_c = (t - CMP_BLOCK) // CMP_STRIDE + 1
    n_s = t // SEL_BLOCK
    rows = t // CMP_STRIDE
    xg = qkv[:, :, nq:nq + 2 * nkv].reshape(bsz, t, 2, g, NSA_HEAD_DIM).transpose(0, 2, 3, 1, 4)
    xg = xg.reshape(bsz, 2, g, rows, CMP_STRIDE * NSA_HEAD_DIM)
    pos = jnp.stack([pos_k, pos_v]).reshape(2, 1, CMP_BLOCK * NSA_HEAD_DIM)
    kvc = nsa_compress(xg, pos, jnp.stack([w1_k, w1_v]).astype(bf16),
                       jnp.stack([w2_k, w2_v]).astype(bf16), n_c)
    cj = np.arange(LANES)[:, None] * CMP_STRIDE
    sk = np.arange(LANES)[None, :] * SEL_BLOCK
    overlap = ((cj < sk + SEL_BLOCK) & (cj + CMP_BLOCK > sk)
               & (np.arange(LANES)[:, None] < n_c) & (np.arange(LANES)[None, :] < n_s))
    o_cmp, sel = nsa_cmp_attention(qkv, kvc, jnp.asarray(overlap.astype(np.float32), bf16), n_c, n_s)
    o_a = nsa_selwin_attention(qkv, sel, o_cmp, gates.reshape(bsz, t, g * LANES))

    lbc = jnp.clip(lb.astype(f32), 0.0, LB_MAX).reshape(1, hd)
    o_b = hgrn2(hg.reshape(bsz, t, 4 * hd), 0, jnp.log(jnp.maximum(lbc, TINY)), jnp.log1p(-lbc),
                1.0 - lbc, hgrn_norm.reshape(1, HGRN_DIM))
    mix = jnp.concatenate([o_a.reshape(m, nq), o_b.reshape(m, hd)], axis=-1)
    return matmul(mix, w_out.astype(bf16)[None], residual=h2, out_dtype=f32, tn=1024)


def _dense_ffn(h2, norm_g, w1, w3, w2):
    u = rmsnorm(h2, norm_g, bf16)
    mid = matmul_gated(u, w1.astype(bf16)[None], w3.astype(bf16)[None], act_a="silu", tn=1408)
    return matmul(mid, w2.astype(bf16)[None], residual=h2, out_dtype=f32)


def _conformer(h2, bsz, t, norm_g, w_pw1, b_pw1, w_dw, b_dw, ln_g, ln_b, w_pw2, b_pw2):
    m, d = h2.shape
    ch = w_dw.shape[-1]
    u = rmsnorm(h2, norm_g, bf16)
    glu = matmul_gated(u, w_pw1[:, :ch].astype(bf16)[None], w_pw1[:, ch:].astype(bf16)[None],
                       ba=b_pw1[:ch].reshape(1, ch), bb=b_pw1[ch:].reshape(1, ch),
                       act_b="sigmoid", out_dtype=f32, tn=1024)
    y = conv_ln_swish(glu.reshape(bsz, t, ch), w_dw, b_dw, ln_g, ln_b)
    return matmul(y.reshape(m, ch), w_pw2.astype(bf16)[None], bias=b_pw2.reshape(1, d),
                  residual=h2, out_dtype=f32, tn=1024)


def _moe(h2, norm_g, w_router, w1, w3, w2, tm=MOE_TM):
    n, d = h2.shape
    e = N_EXPERTS
    s = d // LANES
    u, meta, counts = moe_router(h2, norm_g, w_router)
    tm = min(tm, n)
    idx = meta[:, 0:2].astype(jnp.int32)
    gate = meta[:, 2:4]
    rank = meta[:, 4:6].astype(jnp.int32)
    cnt = counts[0, :e].astype(jnp.int32)
    nblk = (cnt + tm - 1) // tm
    blk_end = jnp.cumsum(nblk)
    pstart = (blk_end - nblk) * tm
    dest = pstart[idx] + rank
    total_blk = (2 * n) // tm + e
    p_rows = total_blk * tm
    blk_e = jnp.minimum(jnp.searchsorted(blk_end, jnp.arange(total_blk), side="right"), e - 1)
    blk_e = blk_e.astype(jnp.int32)
    n_used = blk_end[-1:].astype(jnp.int32)

    xs = moe_dispatch(u.reshape(n, s, LANES), dest, p_rows).reshape(p_rows, d)
    mid = matmul_gated(xs, w1.astype(bf16), w3.astype(bf16), act_a="silu", tm=tm, tn=1024,
                       blk_e=blk_e, n_used=n_used)
    y = matmul(mid, w2.astype(bf16), out_dtype=f32, tm=tm, blk_e=blk_e, n_used=n_used)
    out = moe_combine(y.reshape(p_rows, s, LANES), dest, gate, h2.reshape(n, s, LANES))
    return out.reshape(n, d)


def _lower_bounds(table):
    p = jax.nn.softmax(table.astype(f32), axis=0)
    return jnp.cumsum(p, axis=0) - p[0]


def kernel(x, norm_mix, norm_ffn, final_norm, ab_w_in, ab_w_out, cmp_pos_k, cmp_pos_v, cmp_w1_k, cmp_w1_v, cmp_w2_k, cmp_w2_v, hgrn_lower_bounds, hgrn_norm, ffn_w1, ffn_w3, ffn_w2, conv_w_pw1, conv_b_pw1, conv_w_dw, conv_b_dw, conv_ln_g, conv_ln_b, conv_w_pw2, conv_b_pw2, moe_router, moe_w1, moe_w3, moe_w2):
    bsz, t, d = x.shape
    depth = norm_mix.shape[0]
    lbs = _lower_bounds(hgrn_lower_bounds)
    h = x.reshape(bsz * t, d)
    for layer in range(depth):
        j = layer // 2
        if layer % 2 == 0:
            h = _mixer_ab(h, bsz, t, norm_mix[layer], ab_w_in[j], ab_w_out[j], cmp_pos_k[j],
                          cmp_pos_v[j], cmp_w1_k[j], cmp_w1_v[j], cmp_w2_k[j], cmp_w2_v[j],
                          lbs[j], hgrn_norm[j])
            h = _dense_ffn(h, norm_ffn[layer], ffn_w1[j], ffn_w3[j], ffn_w2[j])
        else:
            h = _conformer(h, bsz, t, norm_mix[layer], conv_w_pw1[j], conv_b_pw1[j], conv_w_dw[j],
                           conv_b_dw[j], conv_ln_g[j], conv_ln_b[j], conv_w_pw2[j], conv_b_pw2[j])
            h = _moe(h, norm_ffn[layer], moe_router[j], moe_w1[j], moe_w3[j], moe_w2[j])
    return rmsnorm(h, final_norm, f32).reshape(bsz, t, d)
```

```python
import functools

import numpy as np
import jax
import jax.numpy as jnp
from jax import lax
from jax.experimental import pallas as pl
from jax.experimental.pallas import tpu as pltpu

f32 = jnp.float32
bf16 = jnp.bfloat16

NSA_HEADS = 8
NSA_KV_GROUPS = 2
NSA_HEAD_DIM = 128
NSA_REP = NSA_HEADS // NSA_KV_GROUPS
CMP_BLOCK = 32
CMP_STRIDE = 16
CMP_HIDDEN = 256
SEL_BLOCK = 64
SEL_TOPN = 8
WINDOW = 512
FORCE_BONUS = 1.0e4
HGRN_HEADS = 8
HGRN_DIM = 128
LB_MAX = 1.0 - 1e-6
CONV_WIDTH = 31
N_EXPERTS = 8
EPS = 1e-6
NEG_BIG = -1e30
TINY = 1e-30
LOWEST = -3.0e38
LOG2E = 1.4426950408889634

LANES = 128
SUBLANES = 8
VMEM_LIMIT = 56 * 1024 * 1024

HGRN_CHUNK = 64
ATT_TQ = 128
ATT_TK = 512
CONV_TT = 256
CONV_HALO = 32
MOE_TM = 512
ROUTER_TM = 512
ROW_TILE = 256


def _cparams(*sem):
    return pltpu.CompilerParams(dimension_semantics=sem, vmem_limit_bytes=VMEM_LIMIT)


def _largest_tile(n, cap, unit):
    if n <= cap:
        return n
    best = max(k for k in range(unit, cap + 1, unit) if n % k == 0)
    return best


def _dot(a, b):
    return jnp.dot(a, b, preferred_element_type=f32)


def _dot_nt(a, b):
    return lax.dot_general(a, b, (((1,), (1,)), ((), ())), preferred_element_type=f32)


def _dot_tn(a, b):
    return lax.dot_general(a, b, (((0,), (0,)), ((), ())), preferred_element_type=f32)


def _sigmoid(x):
    return 1.0 / (1.0 + jnp.exp(-x))


def _silu(x):
    return x * _sigmoid(x)


def _split3(x):
    hi = x.astype(bf16)
    r = x - hi.astype(f32)
    mid = r.astype(bf16)
    lo = (r - mid.astype(f32)).astype(bf16)
    return hi, mid, lo


def _rmsnorm_kernel(x_ref, g_ref, o_ref):
    x = x_ref[...]
    y = x * lax.rsqrt(jnp.mean(x * x, axis=-1, keepdims=True) + EPS) * g_ref[...]
    o_ref[...] = y.astype(o_ref.dtype)


def rmsnorm(x, g, out_dtype, tm=ROW_TILE):
    m, d = x.shape
    return pl.pallas_call(
        _rmsnorm_kernel,
        grid=(m // tm,),
        in_specs=[pl.BlockSpec((tm, d), lambda i: (i, 0)),
                  pl.BlockSpec((1, d), lambda i: (0, 0))],
        out_specs=pl.BlockSpec((tm, d), lambda i: (i, 0)),
        out_shape=jax.ShapeDtypeStruct((m, d), out_dtype),
        compiler_params=_cparams("parallel"),
        name="rmsnorm",
    )(x, g.reshape(1, d))


def _mm_kernel(be_ref, nu_ref, x_ref, w_ref, *rest, has_bias, has_res, n_outer):
    rest = list(rest)
    b_ref = rest.pop(0) if has_bias else None
    r_ref = rest.pop(0) if has_res else None
    o_ref = rest.pop(0)
    i = pl.program_id(1 if n_outer else 0)

    @pl.when(i < nu_ref[0])
    def _():
        acc = _dot(x_ref[...], w_ref[...])
        if has_bias:
            acc = acc + b_ref[...]
        if has_res:
            acc = acc + r_ref[...]
        o_ref[...] = acc.astype(o_ref.dtype)

    @pl.when(i >= nu_ref[0])
    def _():
        o_ref[...] = jnp.zeros(o_ref.shape, o_ref.dtype)


def _mm_setup(m, n, tm, tn, blk_e, n_used, n_outer):
    tm = _largest_tile(m, tm, SUBLANES)
    tn = _largest_tile(n, tn, LANES)
    nblk = m // tm
    if blk_e is None:
        blk_e = jnp.zeros((nblk,), jnp.int32)
        n_used = jnp.full((1,), nblk, jnp.int32)
    if n_outer:
        grid = (n // tn, nblk)

        def ij(f):
            return lambda j, i, be, nu: f(i, j, be, nu)
    else:
        grid = (nblk, n // tn)

        def ij(f):
            return f
    return tm, tn, grid, ij, blk_e, n_used


def _wmap(i, j, be, nu):
    return (be[i], 0, jnp.where(i < nu[0], j, 0))


def matmul(x, w, *, bias=None, residual=None, out_dtype=f32, tm=1024, tn=512,
           blk_e=None, n_used=None, n_outer=False):
    m, k = x.shape
    n = w.shape[-1]
    tm, tn, grid, ij, blk_e, n_used = _mm_setup(m, n, tm, tn, blk_e, n_used, n_outer)
    in_specs = [pl.BlockSpec((tm, k), ij(lambda i, j, be, nu: (i, 0))),
                pl.BlockSpec((None, k, tn), ij(_wmap))]
    args = [x, w]
    if bias is not None:
        in_specs.append(pl.BlockSpec((1, tn), ij(lambda i, j, be, nu: (0, j))))
        args.append(bias)
    if residual is not None:
        in_specs.append(pl.BlockSpec((tm, tn), ij(lambda i, j, be, nu: (i, j))))
        args.append(residual)
    kern = functools.partial(_mm_kernel, has_bias=bias is not None, has_res=residual is not None,
                             n_outer=n_outer)
    return pl.pallas_call(
        kern,
        grid_spec=pltpu.PrefetchScalarGridSpec(
            num_scalar_prefetch=2, grid=grid, in_specs=in_specs,
            out_specs=pl.BlockSpec((tm, tn), ij(lambda i, j, be, nu: (i, j)))),
        out_shape=jax.ShapeDtypeStruct((m, n), out_dtype),
        compiler_params=_cparams("parallel", "arbitrary"),
        name="matmul",
    )(blk_e, n_used, *args)


def _mm2_kernel(be_ref, nu_ref, x_ref, wa_ref, wb_ref, *rest, has_bias, act_a, act_b, n_outer, cast_w):
    rest = list(rest)
    ba_ref = rest.pop(0) if has_bias else None
    bb_ref = rest.pop(0) if has_bias else None
    o_ref = rest.pop(0)
    i = pl.program_id(1 if n_outer else 0)
    active = i < nu_ref[0]
    if cast_w:
        wa16_ref, wb16_ref = rest
        fresh = (i == 0) | (be_ref[i] != be_ref[jnp.maximum(i - 1, 0)])

        @pl.when(active & fresh)
        def _():
            wa16_ref[...] = wa_ref[...].astype(bf16)
            wb16_ref[...] = wb_ref[...].astype(bf16)

        wa_ref, wb_ref = wa16_ref, wb16_ref

    @pl.when(active)
    def _():
        x = x_ref[...]
        a = _dot(x, wa_ref[...])
        b = _dot(x, wb_ref[...])
        if has_bias:
            a = a + ba_ref[...]
            b = b + bb_ref[...]
        if act_a == "silu":
            a = _silu(a)
        if act_b == "sigmoid":
            b = _sigmoid(b)
        o_ref[...] = (a * b).astype(o_ref.dtype)

    @pl.when(jnp.logical_not(active))
    def _():
        o_ref[...] = jnp.zeros(o_ref.shape, o_ref.dtype)


def matmul_gated(x, wa, wb, *, ba=None, bb=None, act_a=None, act_b=None, out_dtype=bf16,
                 tm=1024, tn=512, blk_e=None, n_used=None, n_outer=False):
    m, k = x.shape
    n = wa.shape[-1]
    cast_w = wa.dtype == f32
    assert n_outer or not cast_w
    tm, tn, grid, ij, blk_e, n_used = _mm_setup(m, n, tm, tn, blk_e, n_used, n_outer)
    in_specs = [pl.BlockSpec((tm, k), ij(lambda i, j, be, nu: (i, 0))),
                pl.BlockSpec((None, k, tn), ij(_wmap)),
                pl.BlockSpec((None, k, tn), ij(_wmap))]
    args = [x, wa, wb]
    if ba is not None:
        in_specs += [pl.BlockSpec((1, tn), ij(lambda i, j, be, nu: (0, j)))] * 2
        args += [ba, bb]
    kern = functools.partial(_mm2_kernel, has_bias=ba is not None, act_a=act_a, act_b=act_b,
                             n_outer=n_outer, cast_w=cast_w)
    return pl.pallas_call(
        kern,
        grid_spec=pltpu.PrefetchScalarGridSpec(
            num_scalar_prefetch=2, grid=grid, in_specs=in_specs,
            out_specs=pl.BlockSpec((tm, tn), ij(lambda i, j, be, nu: (i, j))),
            scratch_shapes=[pltpu.VMEM((k, tn), bf16)] * 2 if cast_w else []),
        out_shape=jax.ShapeDtypeStruct((m, n), out_dtype),
        compiler_params=_cparams("parallel", "arbitrary"),
        name="matmul_gated",
    )(blk_e, n_used, *args)


def _gelu_tanh(x):
    c = np.sqrt(2.0 / np.pi).astype(np.float32)
    return 0.5 * x * (1.0 + jnp.tanh(c * (x + 0.044715 * (x * x * x))))


def _compress_kernel(x_ref, pos_ref, w1_ref, w2_ref, o_ref, *, n_c):
    half = CMP_STRIDE * NSA_HEAD_DIM
    x = x_ref[...].astype(f32)
    rows = x.shape[0]
    xn = pltpu.roll(x, rows - 1, 0)
    pos = pos_ref[...]
    xa = (x + pos[:, :half]).astype(bf16)
    xb = (xn + pos[:, half:]).astype(bf16)
    h = _dot(xa, w1_ref[:half, :]) + _dot(xb, w1_ref[half:, :])
    o = _dot(_gelu_tanh(h).astype(bf16), w2_ref[...])
    ridx = lax.broadcasted_iota(jnp.int32, o.shape, 0)
    o_ref[...] = jnp.where(ridx < n_c, o, 0.0).astype(o_ref.dtype)


def nsa_compress(xg, pos, w1, w2, n_c):
    b, _, g, rows, width = xg.shape
    kern = functools.partial(_compress_kernel, n_c=n_c)
    return pl.pallas_call(
        kern,
        grid=(b, 2, g),
        in_specs=[pl.BlockSpec((None, None, None, rows, width), lambda i, s, j: (i, s, j, 0, 0)),
                  pl.BlockSpec((None, 1, 2 * width), lambda i, s, j: (s, 0, 0)),
                  pl.BlockSpec((None, 2 * width, CMP_HIDDEN), lambda i, s, j: (s, 0, 0)),
                  pl.BlockSpec((None, CMP_HIDDEN, NSA_HEAD_DIM), lambda i, s, j: (s, 0, 0))],
        out_specs=pl.BlockSpec((None, None, None, rows, NSA_HEAD_DIM),
                               lambda i, s, j: (i, s, j, 0, 0)),
        out_shape=jax.ShapeDtypeStruct((b, 2, g, rows, NSA_HEAD_DIM), bf16),
        compiler_params=_cparams("parallel", "parallel", "parallel"),
        name="nsa_compress",
    )(xg, pos, w1, w2)


def _head_slopes(rows_head, g):
    h = g * NSA_REP + rows_head
    out = jnp.zeros(h.shape, f32)
    for hh in range(NSA_HEADS):
        out = jnp.where(h == hh, np.float32(2.0 ** (-8.0 * (hh + 1) / NSA_HEADS)), out)
    return out


def _cmp_kernel(q_ref, kc_ref, vc_ref, ov_ref, o_ref, selt_ref, *, n_c, n_s, tq):
    g = pl.program_id(1)
    q0 = pl.program_id(2) * tq
    scale = np.float32(NSA_HEAD_DIM ** -0.5)
    tpos = q0 + lax.broadcasted_iota(jnp.int32, (tq, 1), 0)
    lane = lax.broadcasted_iota(jnp.int32, (1, LANES), 1)
    c_dist = (tpos - (lane * CMP_STRIDE + (CMP_BLOCK - 1))).astype(f32)
    cmask = (c_dist >= 0) & (lane < n_c)
    kc = kc_ref[...]
    vc = vc_ref[...]
    psum = jnp.zeros((tq, LANES), f32)
    for r in range(NSA_REP):
        slope = _head_slopes(jnp.full((1, 1), r, jnp.int32), g)
        qr = q_ref[:, r * NSA_HEAD_DIM:(r + 1) * NSA_HEAD_DIM]
        s = _dot_nt(qr, kc) * scale - slope * c_dist
        s = jnp.where(cmask, s, NEG_BIG)
        m = jnp.max(s, axis=-1, keepdims=True)
        e = jnp.where(cmask, jnp.exp(s - m), 0.0)
        p = e / jnp.maximum(jnp.sum(e, axis=-1, keepdims=True), TINY)
        o_ref[:, r * NSA_HEAD_DIM:(r + 1) * NSA_HEAD_DIM] = _dot(p.astype(bf16), vc).astype(o_ref.dtype)
        psum = psum + p
    hi, mid, lo = _split3(psum)
    ov = ov_ref[...]
    imp = _dot(hi, ov) + _dot(mid, ov) + _dot(lo, ov)
    cur = tpos // SEL_BLOCK
    valid = lane * SEL_BLOCK <= tpos
    forced = valid & ((lane == 0) | (lane == cur) | (lane == cur - 1))
    score = jnp.where(valid, imp, -1.0) + jnp.where(forced, FORCE_BONUS, 0.0)
    score = jnp.where(lane < n_s, score, LOWEST)
    lane_f = lane.astype(f32)
    sel = jnp.zeros((tq, LANES), f32)
    for _ in range(min(SEL_TOPN, n_s)):
        m = jnp.max(score, axis=-1, keepdims=True)
        first = jnp.min(jnp.where(score == m, lane_f, 1e9), axis=-1, keepdims=True)
        hit = lane_f == first
        sel = jnp.where(hit, 1.0, sel)
        score = jnp.where(hit, LOWEST, score)
    selt_ref[...] = sel.T[:selt_ref.shape[0], :]


def nsa_cmp_attention(q, kvc, overlap, n_c, n_s, tq=ROW_TILE):
    b, t, _ = q.shape
    g = NSA_KV_GROUPS
    ncp = kvc.shape[3]
    assert ncp == LANES, "compressed blocks are laid out on one lane tile"
    assert n_s % SUBLANES == 0 and n_s <= LANES
    gw = NSA_REP * NSA_HEAD_DIM
    kern = functools.partial(_cmp_kernel, n_c=n_c, n_s=n_s, tq=tq)
    return pl.pallas_call(
        kern,
        grid=(b, g, t // tq),
        in_specs=[pl.BlockSpec((None, tq, gw), lambda i, j, k: (i, k, j)),
                  pl.BlockSpec((None, None, None, ncp, NSA_HEAD_DIM), lambda i, j, k: (i, 0, j, 0, 0)),
                  pl.BlockSpec((None, None, None, ncp, NSA_HEAD_DIM), lambda i, j, k: (i, 1, j, 0, 0)),
                  pl.BlockSpec((LANES, LANES), lambda i, j, k: (0, 0))],
        out_specs=[pl.BlockSpec((None, tq, gw), lambda i, j, k: (i, k, j)),
                   pl.BlockSpec((None, None, n_s, tq), lambda i, j, k: (i, j, 0, k))],
        out_shape=[jax.ShapeDtypeStruct((b, t, g * gw), bf16),
                   jax.ShapeDtypeStruct((b, g, n_s, t), f32)],
        compiler_params=_cparams("parallel", "parallel", "parallel"),
        name="nsa_cmp",
    )(q, kvc, kvc, overlap)


def _selwin_kernel(q_ref, ks_ref, vs_ref, kw_ref, vw_ref, selt_ref, oc_ref, gt_ref, o_ref, *, tq, tk):
    g = pl.program_id(1)
    q0 = pl.program_id(2) * tq
    rep = NSA_REP
    dh = NSA_HEAD_DIM
    nq = rep * tq
    span = WINDOW + tq
    c_qk = LOG2E * dh ** -0.5
    q4 = jnp.concatenate([q_ref[:, r * dh:(r + 1) * dh] for r in range(rep)], axis=0)
    nslope = [-LOG2E * _head_slopes(jnp.full((1, 1), r, jnp.int32), g) for r in range(rep)]

    def scores(k, dist, ok):
        bias = jnp.concatenate([jnp.where(ok, nslope[r] * dist, NEG_BIG) for r in range(rep)], axis=1)
        return _dot_nt(k, q4) * c_qk + bias

    def rel_pos(rows):
        return (q0 + lax.broadcasted_iota(jnp.int32, (rows, tq), 1)
                - lax.broadcasted_iota(jnp.int32, (rows, tq), 0))

    d_sel = rel_pos(tk)
    per_tile = tk // SEL_BLOCK

    def sel_body(kt, carry):
        m, l, acc = carry
        k0 = pl.multiple_of(kt * tk, tk)
        dist = (d_sel - k0).astype(f32)
        flags = selt_ref[pl.ds(pl.multiple_of(kt * per_tile, per_tile), per_tile), :]
        picked = jnp.concatenate([jnp.broadcast_to(flags[i:i + 1, :], (SEL_BLOCK, tq))
                                  for i in range(per_tile)], axis=0)
        ok = jnp.where(dist >= 0, picked, 0.0) > 0.5
        s = scores(ks_ref[pl.ds(k0, tk), :], dist, ok)
        m_new = jnp.maximum(m, jnp.max(s, axis=0, keepdims=True))
        alpha = jnp.exp2(m - m_new)
        p = jnp.exp2(s - m_new)
        l = alpha * l + jnp.sum(p, axis=0, keepdims=True)
        acc = alpha * acc + _dot_tn(vs_ref[pl.ds(k0, tk), :], p.astype(bf16))
        return m_new, l, acc

    init = (jnp.full((1, nq), NEG_BIG, f32), jnp.zeros((1, nq), f32), jnp.zeros((dh, nq), f32))
    _, l, acc = lax.fori_loop(0, (q0 + tq + tk - 1) // tk, sel_body, init)
    ot_sel = acc / jnp.maximum(l, TINY)

    k_lo = pl.multiple_of(jnp.maximum(q0 - WINDOW, 0), tq)
    dist = (rel_pos(span) - k_lo).astype(f32)
    ok = jnp.abs(dist - 0.5 * (WINDOW - 1)) <= 0.5 * (WINDOW - 1)
    s = scores(kw_ref[pl.ds(k_lo, span), :], dist, ok)
    m = jnp.max(s, axis=0, keepdims=True)
    p = jnp.exp2(s - m)
    l = jnp.sum(p, axis=0, keepdims=True)
    ot_win = _dot_tn(vw_ref[pl.ds(k_lo, span), :], p.astype(bf16)) / jnp.maximum(l, TINY)

    gate = _sigmoid(gt_ref[...])
    for r in range(rep):
        oc = oc_ref[:, r * dh:(r + 1) * dh].astype(f32)
        os_ = ot_sel[:, r * tq:(r + 1) * tq].T
        ow = ot_win[:, r * tq:(r + 1) * tq].T
        o = (gate[:, r:r + 1] * oc + gate[:, rep + r:rep + r + 1] * os_
             + gate[:, 2 * rep + r:2 * rep + r + 1] * ow)
        o_ref[:, r * dh:(r + 1) * dh] = o.astype(o_ref.dtype)


def nsa_selwin_attention(qkv, sel, o_cmp, gates, tq=ATT_TQ, tk=ATT_TK):
    b, t, _ = qkv.shape
    g = NSA_KV_GROUPS
    gw = NSA_REP * NSA_HEAD_DIM
    n_s = sel.shape[2]
    assert t % tk == 0 and t % tq == 0 and WINDOW % tq == 0 and t >= WINDOW + tq
    assert tk % (SEL_BLOCK * SUBLANES) == 0 and n_s * SEL_BLOCK == t

    def kvspec(branch):
        return pl.BlockSpec((None, t, NSA_HEAD_DIM),
                            lambda i, j, k: (i, 0, NSA_HEADS + branch * g + j))

    kern = functools.partial(_selwin_kernel, tq=tq, tk=tk)
    return pl.pallas_call(
        kern,
        grid=(b, g, t // tq),
        in_specs=[pl.BlockSpec((None, tq, gw), lambda i, j, k: (i, k, j)),
                  kvspec(2), kvspec(3), kvspec(4), kvspec(5),
                  pl.BlockSpec((None, None, n_s, tq), lambda i, j, k: (i, j, 0, k)),
                  pl.BlockSpec((None, tq, gw), lambda i, j, k: (i, k, j)),
                  pl.BlockSpec((None, tq, LANES), lambda i, j, k: (i, k, j))],
        out_specs=pl.BlockSpec((None, tq, gw), lambda i, j, k: (i, k, j)),
        out_shape=jax.ShapeDtypeStruct(o_cmp.shape, bf16),
        compiler_params=_cparams("parallel", "parallel", "parallel"),
        name="nsa_selwin",
    )(qkv, qkv, qkv, qkv, qkv, sel, o_cmp, gates)


def _hgrn_tables(c):
    levels = []
    m = c // 2
    while m >= 1:
        levels.append(m)
        m //= 2
    t = np.arange(c)[:, None]
    u = np.arange(c)[None, :]
    mats = [(u <= t), (u > t)]
    masks = []
    for m in levels:
        ref = (t // (2 * m)) * 2 * m + m - 1
        right = (t % (2 * m)) >= m
        mats.append(u <= ref)
        s = np.arange(c)[None, :]
        masks.append((t // (2 * m) == s // (2 * m)) & right & ((s % (2 * m)) < m))
    lstack = np.concatenate([x.astype(np.float32) for x in mats], axis=0)
    return levels, lstack, np.stack([x.astype(np.float32) for x in masks])


def _hgrn_kernel(q_ref, f_ref, i_ref, g_ref, la_ref, lc_ref, oml_ref, ng_ref, ls_ref, mk_ref,
                 o_ref, expo_ref, st_ref, *, c, n_levels):
    d = HGRN_DIM

    @pl.when(pl.program_id(1) == 0)
    def _():
        st_ref[...] = jnp.zeros(st_ref.shape, f32)

    z = f_ref[...]
    log_sig = jnp.minimum(z, 0.0) - jnp.log(1.0 + jnp.exp(-jnp.abs(z)))
    y = lc_ref[...] + log_sig
    a = la_ref[...]
    log_f = jnp.maximum(a, y) + jnp.log(1.0 + jnp.exp(-jnp.abs(a - y)))
    kk = oml_ref[...] * (1.0 / (1.0 + jnp.exp(z)))
    qf = _silu(q_ref[...])
    v = i_ref[...]
    gt = _silu(g_ref[...])
    hi, mid, lo = _split3(log_f)
    ls = ls_ref[...]
    expo_ref[...] = _dot(ls, hi) + _dot(ls, mid) + _dot(ls, lo)
    ng = ng_ref[...]
    for h in range(HGRN_HEADS):
        sl = slice(h * d, (h + 1) * d)
        b = expo_ref[0:c, sl]
        tail = expo_ref[c:2 * c, sl]
        qh = qf[:, sl]
        kh = kk[:, sl]
        vh = v[:, sl]
        vb = vh.astype(bf16)
        st = st_ref[h]
        o = _dot_nt((qh * jnp.exp(b)).astype(bf16), st.astype(bf16))
        amat = jnp.zeros((c, c), f32)
        for li in range(n_levels):
            e = jnp.exp(-jnp.abs(b - expo_ref[(2 + li) * c:(3 + li) * c, sl]))
            part = _dot_nt((qh * e).astype(bf16), (kh * e).astype(bf16))
            amat = amat + mk_ref[li] * part
        diag = jnp.sum(qh * kh, axis=-1, keepdims=True)
        o = o + _dot(amat.astype(bf16), vb) + diag * vh
        st_ref[h] = st * jnp.exp(b[c - 1:c, :]) + _dot_tn(vb, (kh * jnp.exp(tail)).astype(bf16))
        o = o * lax.rsqrt(jnp.mean(o * o, axis=-1, keepdims=True) + EPS) * ng
        o_ref[:, sl] = (o * gt[:, sl]).astype(o_ref.dtype)


def hgrn2(proj, col0, log_lb, log1m_lb, one_m_lb, norm_g, c=HGRN_CHUNK):
    b, t, _ = proj.shape
    hd = HGRN_HEADS * HGRN_DIM
    levels, lstack, masks = _hgrn_tables(c)
    cb = col0 // hd
    assert col0 % hd == 0 and t % c == 0
    nl = len(levels)
    kern = functools.partial(_hgrn_kernel, c=c, n_levels=nl)

    def seg(k):
        return pl.BlockSpec((None, c, hd), lambda i, j: (i, j, cb + k))

    def row(width):
        return pl.BlockSpec((1, width), lambda i, j: (0, 0))

    return pl.pallas_call(
        kern,
        grid=(b, t // c),
        in_specs=[seg(0), seg(1), seg(2), seg(3), row(hd), row(hd), row(hd), row(HGRN_DIM),
                  pl.BlockSpec(lstack.shape, lambda i, j: (0, 0)),
                  pl.BlockSpec(masks.shape, lambda i, j: (0, 0, 0))],
        out_specs=pl.BlockSpec((None, c, hd), lambda i, j: (i, j, 0)),
        out_shape=jax.ShapeDtypeStruct((b, t, hd), bf16),
        scratch_shapes=[pltpu.VMEM((lstack.shape[0], hd), f32),
                        pltpu.VMEM((HGRN_HEADS, HGRN_DIM, HGRN_DIM), f32)],
        compiler_params=_cparams("parallel", "arbitrary"),
        name="hgrn2",
    )(proj, proj, proj, proj, log_lb, log1m_lb, one_m_lb, norm_g,
      jnp.asarray(lstack, bf16), jnp.asarray(masks, f32))


def _conv_kernel(x_ref, w_ref, b_ref, g_ref, bb_ref, o_ref, buf_ref, acc_ref, *, tt, sub):
    halo = CONV_HALO
    ch = x_ref.shape[-1]

    @pl.when(pl.program_id(1) == 0)
    def _():
        buf_ref[0:halo, :] = jnp.zeros((halo, ch), f32)

    buf_ref[halo:halo + tt, :] = x_ref[...]
    base = halo - (CONV_WIDTH - 1)

    def lane_block(cb, carry):
        c0 = pl.multiple_of(cb * LANES, LANES)
        w = w_ref[:, pl.ds(c0, LANES)]
        for ts in range(tt // sub):
            acc = jnp.zeros((sub, LANES), f32) + b_ref[:, pl.ds(c0, LANES)]
            for phase in range(SUBLANES):
                taps = [j for j in range(CONV_WIDTH) if (base + j) % SUBLANES == phase]
                reach = max((base + j) // SUBLANES for j in taps) * SUBLANES
                if phase == 0:
                    slab = buf_ref[pl.ds(ts * sub, sub + reach), pl.ds(c0, LANES)]
                else:
                    rows = sub + reach + SUBLANES
                    slab = pltpu.roll(buf_ref[pl.ds(ts * sub, rows), pl.ds(c0, LANES)], rows - phase, 0)
                for j in taps:
                    off = (base + j) // SUBLANES * SUBLANES
                    acc = acc + w[j:j + 1, :] * slab[off:off + sub, :]
            acc_ref[ts * sub:(ts + 1) * sub, pl.ds(c0, LANES)] = acc
        return carry

    lax.fori_loop(0, ch // LANES, lane_block, 0)
    buf_ref[0:halo, :] = buf_ref[tt:tt + halo, :]
    y = acc_ref[...]
    mu = jnp.mean(y, axis=-1, keepdims=True)
    yc = y - mu
    var = jnp.mean(yc * yc, axis=-1, keepdims=True)
    yn = yc * lax.rsqrt(var + EPS) * g_ref[...] + bb_ref[...]
    o_ref[...] = _silu(yn).astype(o_ref.dtype)


def conv_ln_swish(x, w_dw, b_dw, ln_g, ln_b, tt=CONV_TT, sub=64):
    b, t, ch = x.shape
    tt = min(tt, t)
    assert t % tt == 0 and tt % sub == 0 and tt >= CONV_HALO
    wpad = jnp.zeros((CONV_HALO, ch), f32).at[:CONV_WIDTH].set(w_dw)
    kern = functools.partial(_conv_kernel, tt=tt, sub=sub)

    def row():
        return pl.BlockSpec((1, ch), lambda i, j: (0, 0))

    return pl.pallas_call(
        kern,
        grid=(b, t // tt),
        in_specs=[pl.BlockSpec((None, tt, ch), lambda i, j: (i, j, 0)),
                  pl.BlockSpec((CONV_HALO, ch), lambda i, j: (0, 0)), row(), row(), row()],
        out_specs=pl.BlockSpec((None, tt, ch), lambda i, j: (i, j, 0)),
        out_shape=jax.ShapeDtypeStruct((b, t, ch), bf16),
        scratch_shapes=[pltpu.VMEM((tt + CONV_HALO, ch), f32), pltpu.VMEM((tt, ch), f32)],
        compiler_params=_cparams("parallel", "arbitrary"),
        name="conv_ln_swish",
    )(x, wpad, b_dw.reshape(1, ch), ln_g.reshape(1, ch), ln_b.reshape(1, ch))


def _router_kernel(x_ref, g_ref, w_ref, tri_ref, u_ref, meta_ref, cnt_ref, run_ref):
    @pl.when(pl.program_id(0) == 0)
    def _():
        run_ref[...] = jnp.zeros(run_ref.shape, f32)

    x = x_ref[...]
    u = x * lax.rsqrt(jnp.mean(x * x, axis=-1, keepdims=True) + EPS) * g_ref[...]
    u_ref[...] = u.astype(u_ref.dtype)
    uh, um, _ = _split3(u)
    wh, wm, _ = _split3(w_ref[...])
    logits = _dot(uh, wh) + (_dot(uh, wm) + _dot(um, wh))
    lane = lax.broadcasted_iota(jnp.int32, (1, LANES), 1)
    lane_f = lane.astype(f32)
    l1 = jnp.where(lane < N_EXPERTS, logits, LOWEST)
    m1 = jnp.max(l1, axis=-1, keepdims=True)
    i1 = jnp.min(jnp.where(l1 == m1, lane_f, 1e9), axis=-1, keepdims=True)
    l2 = jnp.where(lane_f == i1, LOWEST, l1)
    m2 = jnp.max(l2, axis=-1, keepdims=True)
    i2 = jnp.min(jnp.where(l2 == m2, lane_f, 1e9), axis=-1, keepdims=True)
    e2 = jnp.exp(m2 - m1)
    g1 = 1.0 / (1.0 + e2)
    g2 = e2 / (1.0 + e2)
    hit1 = lane_f == i1
    hit2 = lane_f == i2
    onehot = jnp.where(hit1 | hit2, 1.0, 0.0)
    cum = _dot(tri_ref[...], onehot.astype(bf16)) + run_ref[...]
    r1 = jnp.sum(jnp.where(hit1, cum, 0.0), axis=-1, keepdims=True)
    r2 = jnp.sum(jnp.where(hit2, cum, 0.0), axis=-1, keepdims=True)
    run = run_ref[...] + jnp.sum(onehot, axis=0, keepdims=True)
    run_ref[...] = run
    cnt_ref[...] = run
    meta = jnp.zeros(meta_ref.shape, f32)
    for col, val in enumerate((i1, i2, g1, g2, r1, r2)):
        meta = jnp.where(lane == col, val, meta)
    meta_ref[...] = meta


def moe_router(h, norm_g, w_router, tm=ROUTER_TM):
    m, d = h.shape
    tm = min(tm, m)
    wpad = jnp.zeros((d, LANES), f32).at[:, :N_EXPERTS].set(w_router)
    tri = jnp.asarray(np.tril(np.ones((tm, tm), np.float32), -1), bf16)
    return pl.pallas_call(
        _router_kernel,
        grid=(m // tm,),
        in_specs=[pl.BlockSpec((tm, d), lambda i: (i, 0)),
                  pl.BlockSpec((1, d), lambda i: (0, 0)),
                  pl.BlockSpec((d, LANES), lambda i: (0, 0)),
                  pl.BlockSpec((tm, tm), lambda i: (0, 0))],
        out_specs=[pl.BlockSpec((tm, d), lambda i: (i, 0)),
                   pl.BlockSpec((tm, LANES), lambda i: (i, 0)),
                   pl.BlockSpec((1, LANES), lambda i: (0, 0))],
        out_shape=[jax.ShapeDtypeStruct((m, d), bf16),
                   jax.ShapeDtypeStruct((m, LANES), f32),
                   jax.ShapeDtypeStruct((1, LANES), f32)],
        scratch_shapes=[pltpu.VMEM((1, LANES), f32)],
        compiler_params=_cparams("arbitrary"),
        name="moe_router",
    )(h, norm_g.reshape(1, d), wpad, tri)


def _dispatch_kernel(dest_ref, u_ref, xs_in_ref, xs_ref, sem, *, tm):
    del xs_in_ref

    def issue(r, carry):
        for k in range(2):
            pltpu.make_async_copy(u_ref.at[r], xs_ref.at[dest_ref[0, 2 * r + k]], sem).start()
        return carry

    lax.fori_loop(0, tm, issue, 0)

    def drain(r, carry):
        for k in range(2):
            pltpu.make_async_copy(u_ref.at[0], xs_ref.at[0], sem).wait()
        return carry

    lax.fori_loop(0, tm, drain, 0)


def moe_dispatch(u3, dest, p_rows, tm=ROW_TILE):
    n, s, _ = u3.shape
    tm = min(tm, n)
    xs0 = jnp.zeros((p_rows, s, LANES), u3.dtype)
    kern = functools.partial(_dispatch_kernel, tm=tm)
    return pl.pallas_call(
        kern,
        grid=(n // tm,),
        in_specs=[pl.BlockSpec((None, 1, 2 * tm), lambda i: (i, 0, 0), memory_space=pltpu.SMEM),
                  pl.BlockSpec((tm, s, LANES), lambda i: (i, 0, 0)),
                  pl.BlockSpec(memory_space=pl.ANY)],
        out_specs=pl.BlockSpec(memory_space=pl.ANY),
        out_shape=jax.ShapeDtypeStruct(xs0.shape, xs0.dtype),
        scratch_shapes=[pltpu.SemaphoreType.DMA(())],
        input_output_aliases={2: 0},
        compiler_params=_cparams("arbitrary"),
        name="moe_dispatch",
    )(dest.reshape(n // tm, 1, 2 * tm), u3, xs0)


def _combine_kernel(dest_ref, gate_ref, y_ref, h_ref, o_ref, buf_ref, sem, *, tm):
    def issue(r, carry):
        for k in range(2):
            pltpu.make_async_copy(y_ref.at[dest_ref[0, 2 * r + k]], buf_ref.at[k, r], sem).start()
        return carry

    lax.fori_loop(0, tm, issue, 0)

    def drain(r, carry):
        for k in range(2):
            pltpu.make_async_copy(y_ref.at[0], buf_ref.at[0, 0], sem).wait()
        return carry

    lax.fori_loop(0, tm, drain, 0)

    def mix(r, carry):
        o_ref[r] = (h_ref[r] + gate_ref[0, 2 * r] * buf_ref[0, r]
                    + gate_ref[0, 2 * r + 1] * buf_ref[1, r])
        return carry

    lax.fori_loop(0, tm, mix, 0)


def moe_combine(y3, dest, gate, h3, tm=ROW_TILE):
    n, s, _ = h3.shape
    tm = min(tm, n)
    kern = functools.partial(_combine_kernel, tm=tm)
    return pl.pallas_call(
        kern,
        grid=(n // tm,),
        in_specs=[pl.BlockSpec((None, 1, 2 * tm), lambda i: (i, 0, 0), memory_space=pltpu.SMEM),
                  pl.BlockSpec((None, 1, 2 * tm), lambda i: (i, 0, 0), memory_space=pltpu.SMEM),
                  pl.BlockSpec(memory_space=pl.ANY),
                  pl.BlockSpec((tm, s, LANES), lambda i: (i, 0, 0))],
        out_specs=pl.BlockSpec((tm, s, LANES), lambda i: (i, 0, 0)),
        out_shape=jax.ShapeDtypeStruct(h3.shape, f32),
        scratch_shapes=[pltpu.VMEM((2, tm, s, LANES), f32), pltpu.SemaphoreType.DMA(())],
        compiler_params=_cparams("arbitrary"),
        name="moe_combine",
    )(dest.reshape(n // tm, 1, 2 * tm), gate.reshape(n // tm, 1, 2 * tm), y3, h3)


def _mixer_ab(h2, bsz, t, norm_g, w_in, w_out, pos_k, pos_v, w1_k, w1_v, w2_k, w2_v, lb, hgrn_norm):
    m, d = h2.shape
    nq = NSA_HEADS * NSA_HEAD_DIM
    nkv = NSA_KV_GROUPS * NSA_HEAD_DIM
    hd = HGRN_HEADS * HGRN_DIM
    g, rep = NSA_KV_GROUPS, NSA_REP
    u = rmsnorm(h2, norm_g, bf16)
    o_kv = nq
    o_gate = o_kv + 6 * nkv
    o_hg = o_gate + 3 * NSA_HEADS
    w_attn = w_in[:, :o_gate].astype(bf16)[None]
    w_hg = w_in[:, o_hg:o_hg + 4 * hd].astype(bf16)[None]
    wg = w_in[:, o_gate:o_hg].reshape(d, g, rep, 3).transpose(0, 1, 3, 2).reshape(d, g, 3 * rep)
    wg = jnp.pad(wg, ((0, 0), (0, 0), (0, LANES - 3 * rep))).reshape(d, g * LANES).astype(bf16)[None]

    attn = matmul(u, w_attn, out_dtype=bf16)
    hg = matmul(u, w_hg, out_dtype=f32, tn=1024)
    gates = matmul(u, wg, out_dtype=f32, tn=g * LANES)

    qkv = attn.reshape(bsz, t, nq + 6 * nkv)
    n_c = (t - CMP_BLOCK) // CMP_STRIDE + 1
    n_s = t // SEL_BLOCK
    rows = t // CMP_STRIDE
    xg = qkv[:, :, nq:nq + 2 * nkv].reshape(bsz, t, 2, g, NSA_HEAD_DIM).transpose(0, 2, 3, 1, 4)
    xg = xg.reshape(bsz, 2, g, rows, CMP_STRIDE * NSA_HEAD_DIM)
    pos = jnp.stack([pos_k, pos_v]).reshape(2, 1, CMP_BLOCK * NSA_HEAD_DIM)
    kvc = nsa_compress(xg, pos, jnp.stack([w1_k, w1_v]).astype(bf16),
                       jnp.stack([w2_k, w2_v]).astype(bf16), n_c)
    cj = np.arange(LANES)[:, None] * CMP_STRIDE
    sk = np.arange(LANES)[None, :] * SEL_BLOCK
    overlap = ((cj < sk + SEL_BLOCK) & (cj + CMP_BLOCK > sk)
               & (np.arange(LANES)[:, None] < n_c) & (np.arange(LANES)[None, :] < n_s))
    o_cmp, sel = nsa_cmp_attention(qkv, kvc, jnp.asarray(overlap.astype(np.float32), bf16), n_c, n_s)
    o_a = nsa_selwin_attention(qkv, sel, o_cmp, gates.reshape(bsz, t, g * LANES))

    lbc = jnp.clip(lb.astype(f32), 0.0, LB_MAX).reshape(1, hd)
    o_b = hgrn2(hg.reshape(bsz, t, 4 * hd), 0, jnp.log(jnp.maximum(lbc, TINY)), jnp.log1p(-lbc),
                1.0 - lbc, hgrn_norm.reshape(1, HGRN_DIM))
    mix = jnp.concatenate([o_a.reshape(m, nq), o_b.reshape(m, hd)], axis=-1)
    return matmul(mix, w_out.astype(bf16)[None], residual=h2, out_dtype=f32, tn=1024)


def _dense_ffn(h2, norm_g, w1_all, w3_all, w2, j, tm=1024):
    u = rmsnorm(h2, norm_g, bf16)
    tm = _largest_tile(h2.shape[0], tm, SUBLANES)
    nblk = h2.shape[0] // tm
    mid = matmul_gated(u, w1_all, w3_all, act_a="silu", tm=tm, n_outer=True,
                       blk_e=jnp.full((nblk,), j, jnp.int32), n_used=jnp.full((1,), nblk, jnp.int32))
    return matmul(mid, w2.astype(bf16)[None], residual=h2, out_dtype=f32)


def _conformer(h2, bsz, t, norm_g, w_pw1, b_pw1, w_dw, b_dw, ln_g, ln_b, w_pw2, b_pw2):
    m, d = h2.shape
    ch = w_dw.shape[-1]
    u = rmsnorm(h2, norm_g, bf16)
    glu = matmul_gated(u, w_pw1[:, :ch].astype(bf16)[None], w_pw1[:, ch:].astype(bf16)[None],
                       ba=b_pw1[:ch].reshape(1, ch), bb=b_pw1[ch:].reshape(1, ch),
                       act_b="sigmoid", out_dtype=f32, tn=1024)
    y = conv_ln_swish(glu.reshape(bsz, t, ch), w_dw, b_dw, ln_g, ln_b)
    return matmul(y.reshape(m, ch), w_pw2.astype(bf16)[None], bias=b_pw2.reshape(1, d),
                  residual=h2, out_dtype=f32, tn=1024)


def _moe(h2, norm_g, w_router, w1_all, w3_all, w2, j, tm=MOE_TM):
    n, d = h2.shape
    e = N_EXPERTS
    s = d // LANES
    u, meta, counts = moe_router(h2, norm_g, w_router)
    tm = min(tm, n)
    idx = meta[:, 0:2].astype(jnp.int32)
    gate = meta[:, 2:4]
    rank = meta[:, 4:6].astype(jnp.int32)
    cnt = counts[0, :e].astype(jnp.int32)
    nblk = (cnt + tm - 1) // tm
    blk_end = jnp.cumsum(nblk)
    pstart = (blk_end - nblk) * tm
    dest = pstart[idx] + rank
    total_blk = (2 * n) // tm + e
    p_rows = total_blk * tm
    blk_e = jnp.minimum(jnp.searchsorted(blk_end, jnp.arange(total_blk), side="right"), e - 1)
    blk_e = blk_e.astype(jnp.int32)
    n_used = blk_end[-1:].astype(jnp.int32)

    xs = moe_dispatch(u.reshape(n, s, LANES), dest, p_rows).reshape(p_rows, d)
    mid = matmul_gated(xs, w1_all, w3_all, act_a="silu", tm=tm, tn=1024, n_outer=True,
                       blk_e=blk_e + j * e, n_used=n_used)
    y = matmul(mid, w2.astype(bf16), out_dtype=f32, tm=tm, blk_e=blk_e, n_used=n_used, n_outer=True)
    out = moe_combine(y.reshape(p_rows, s, LANES), dest, gate, h2.reshape(n, s, LANES))
    return out.reshape(n, d)


def _lower_bounds(table):
    p = jax.nn.softmax(table.astype(f32), axis=0)
    return jnp.cumsum(p, axis=0) - p[0]


def kernel(x, norm_mix, norm_ffn, final_norm, ab_w_in, ab_w_out, cmp_pos_k, cmp_pos_v, cmp_w1_k, cmp_w1_v, cmp_w2_k, cmp_w2_v, hgrn_lower_bounds, hgrn_norm, ffn_w1, ffn_w3, ffn_w2, conv_w_pw1, conv_b_pw1, conv_w_dw, conv_b_dw, conv_ln_g, conv_ln_b, conv_w_pw2, conv_b_pw2, moe_router, moe_w1, moe_w3, moe_w2):
    bsz, t, d = x.shape
    depth = norm_mix.shape[0]
    lbs = _lower_bounds(hgrn_lower_bounds)
    h = x.reshape(bsz * t, d)
    for layer in range(depth):
        j = layer // 2
        if layer % 2 == 0:
            h = _mixer_ab(h, bsz, t, norm_mix[layer], ab_w_in[j], ab_w_out[j], cmp_pos_k[j],
                          cmp_pos_v[j], cmp_w1_k[j], cmp_w1_v[j], cmp_w2_k[j], cmp_w2_v[j],
                          lbs[j], hgrn_norm[j])
            h = _dense_ffn(h, norm_ffn[layer], ffn_w1, ffn_w3, ffn_w2[j], j)
        else:
            h = _conformer(h, bsz, t, norm_mix[layer], conv_w_pw1[j], conv_b_pw1[j], conv_w_dw[j],
                           conv_b_dw[j], conv_ln_g[j], conv_ln_b[j], conv_w_pw2[j], conv_b_pw2[j])
            fe = moe_w1.shape[-1]
            h = _moe(h, norm_ffn[layer], moe_router[j], moe_w1.reshape(-1, d, fe),
                     moe_w3.reshape(-1, d, fe), moe_w2[j], j)
    return rmsnorm(h, final_norm, f32).reshape(bsz, t, d)
```

```python
import functools

import numpy as np
import jax
import jax.numpy as jnp
from jax import lax
from jax.experimental import pallas as pl
from jax.experimental.pallas import tpu as pltpu

f32 = jnp.float32
bf16 = jnp.bfloat16

NSA_HEADS = 8
NSA_KV_GROUPS = 2
NSA_HEAD_DIM = 128
NSA_REP = NSA_HEADS // NSA_KV_GROUPS
CMP_BLOCK = 32
CMP_STRIDE = 16
CMP_HIDDEN = 256
SEL_BLOCK = 64
SEL_TOPN = 8
WINDOW = 512
FORCE_BONUS = 1.0e4
HGRN_HEADS = 8
HGRN_DIM = 128
LB_MAX = 1.0 - 1e-6
CONV_WIDTH = 31
N_EXPERTS = 8
EPS = 1e-6
NEG_BIG = -1e30
TINY = 1e-30
LOWEST = -3.0e38
LOG2E = 1.4426950408889634

LANES = 128
SUBLANES = 8
VMEM_LIMIT = 56 * 1024 * 1024

HGRN_CHUNK = 128
HGRN_BCAST_LEVELS = 2
ATT_TQ = 128
ATT_TK = 512
CONV_TT = 256
CONV_HALO = 32
MOE_TM = 512
ROUTER_TM = 512
ROW_TILE = 256


def _cparams(*sem):
    return pltpu.CompilerParams(dimension_semantics=sem, vmem_limit_bytes=VMEM_LIMIT)


def _largest_tile(n, cap, unit):
    if n <= cap:
        return n
    best = max(k for k in range(unit, cap + 1, unit) if n % k == 0)
    return best


def _dot(a, b):
    return jnp.dot(a, b, preferred_element_type=f32)


def _dot_nt(a, b):
    return lax.dot_general(a, b, (((1,), (1,)), ((), ())), preferred_element_type=f32)


def _dot_tn(a, b):
    return lax.dot_general(a, b, (((0,), (0,)), ((), ())), preferred_element_type=f32)


def _sigmoid(x):
    return 1.0 / (1.0 + jnp.exp(-x))


def _silu(x):
    return x * _sigmoid(x)


def _split3(x):
    hi = x.astype(bf16)
    r = x - hi.astype(f32)
    mid = r.astype(bf16)
    lo = (r - mid.astype(f32)).astype(bf16)
    return hi, mid, lo


def _rmsnorm_kernel(x_ref, g_ref, o_ref):
    x = x_ref[...]
    y = x * lax.rsqrt(jnp.mean(x * x, axis=-1, keepdims=True) + EPS) * g_ref[...]
    o_ref[...] = y.astype(o_ref.dtype)


def rmsnorm(x, g, out_dtype, tm=ROW_TILE):
    m, d = x.shape
    return pl.pallas_call(
        _rmsnorm_kernel,
        grid=(m // tm,),
        in_specs=[pl.BlockSpec((tm, d), lambda i: (i, 0)),
                  pl.BlockSpec((1, d), lambda i: (0, 0))],
        out_specs=pl.BlockSpec((tm, d), lambda i: (i, 0)),
        out_shape=jax.ShapeDtypeStruct((m, d), out_dtype),
        compiler_params=_cparams("parallel"),
        name="rmsnorm",
    )(x, g.reshape(1, d))


def _mm_kernel(be_ref, nu_ref, x_ref, w_ref, *rest, has_bias, has_res, n_outer, cast_w):
    rest = list(rest)
    b_ref = rest.pop(0) if has_bias else None
    r_ref = rest.pop(0) if has_res else None
    o_ref = rest.pop(0)
    i = pl.program_id(1 if n_outer else 0)
    if cast_w:
        w16_ref, = rest
        fresh = (i == 0) | (be_ref[i] != be_ref[jnp.maximum(i - 1, 0)])

        @pl.when((i < nu_ref[0]) & fresh)
        def _():
            w16_ref[...] = w_ref[...].astype(bf16)

        w_ref = w16_ref

    @pl.when(i < nu_ref[0])
    def _():
        acc = _dot(x_ref[...], w_ref[...])
        if has_bias:
            acc = acc + b_ref[...]
        if has_res:
            acc = acc + r_ref[...]
        o_ref[...] = acc.astype(o_ref.dtype)

    @pl.when(i >= nu_ref[0])
    def _():
        o_ref[...] = jnp.zeros(o_ref.shape, o_ref.dtype)


def _mm_setup(m, n, tm, tn, blk_e, n_used, n_outer):
    tm = _largest_tile(m, tm, SUBLANES)
    tn = _largest_tile(n, tn, LANES)
    nblk = m // tm
    if blk_e is None:
        blk_e = jnp.zeros((nblk,), jnp.int32)
        n_used = jnp.full((1,), nblk, jnp.int32)
    if n_outer:
        grid = (n // tn, nblk)

        def ij(f):
            return lambda j, i, be, nu: f(i, j, be, nu)
    else:
        grid = (nblk, n // tn)

        def ij(f):
            return f
    return tm, tn, grid, ij, blk_e, n_used


def _wmap(i, j, be, nu):
    return (be[i], 0, jnp.where(i < nu[0], j, 0))


def matmul(x, w, *, bias=None, residual=None, out_dtype=f32, tm=1024, tn=512,
           blk_e=None, n_used=None, n_outer=False):
    m, k = x.shape
    n = w.shape[-1]
    cast_w = w.dtype == f32
    assert n_outer or not cast_w
    tm, tn, grid, ij, blk_e, n_used = _mm_setup(m, n, tm, tn, blk_e, n_used, n_outer)
    w_mode = dict(pipeline_mode=pl.Buffered(1)) if cast_w else {}
    in_specs = [pl.BlockSpec((tm, k), ij(lambda i, j, be, nu: (i, 0))),
                pl.BlockSpec((None, k, tn), ij(_wmap), **w_mode)]
    args = [x, w]
    if bias is not None:
        in_specs.append(pl.BlockSpec((1, tn), ij(lambda i, j, be, nu: (0, j))))
        args.append(bias)
    if residual is not None:
        in_specs.append(pl.BlockSpec((tm, tn), ij(lambda i, j, be, nu: (i, j))))
        args.append(residual)
    kern = functools.partial(_mm_kernel, has_bias=bias is not None, has_res=residual is not None,
                             n_outer=n_outer, cast_w=cast_w)
    return pl.pallas_call(
        kern,
        grid_spec=pltpu.PrefetchScalarGridSpec(
            num_scalar_prefetch=2, grid=grid, in_specs=in_specs,
            out_specs=pl.BlockSpec((tm, tn), ij(lambda i, j, be, nu: (i, j))),
            scratch_shapes=[pltpu.VMEM((k, tn), bf16)] if cast_w else []),
        out_shape=jax.ShapeDtypeStruct((m, n), out_dtype),
        compiler_params=_cparams("parallel", "arbitrary"),
        name="matmul",
    )(blk_e, n_used, *args)


def _mm2_kernel(be_ref, nu_ref, x_ref, wa_ref, wb_ref, *rest, has_bias, act_a, act_b, n_outer, cast_w):
    rest = list(rest)
    ba_ref = rest.pop(0) if has_bias else None
    bb_ref = rest.pop(0) if has_bias else None
    o_ref = rest.pop(0)
    i = pl.program_id(1 if n_outer else 0)
    active = i < nu_ref[0]
    if cast_w:
        wa16_ref, wb16_ref = rest
        fresh = (i == 0) | (be_ref[i] != be_ref[jnp.maximum(i - 1, 0)])

        @pl.when(active & fresh)
        def _():
            wa16_ref[...] = wa_ref[...].astype(bf16)
            wb16_ref[...] = wb_ref[...].astype(bf16)

        wa_ref, wb_ref = wa16_ref, wb16_ref

    @pl.when(active)
    def _():
        x = x_ref[...]
        a = _dot(x, wa_ref[...])
        b = _dot(x, wb_ref[...])
        if has_bias:
            a = a + ba_ref[...]
            b = b + bb_ref[...]
        if act_a == "silu":
            a = _silu(a)
        if act_b == "sigmoid":
            b = _sigmoid(b)
        o_ref[...] = (a * b).astype(o_ref.dtype)

    @pl.when(jnp.logical_not(active))
    def _():
        o_ref[...] = jnp.zeros(o_ref.shape, o_ref.dtype)


def matmul_gated(x, wa, wb, *, ba=None, bb=None, act_a=None, act_b=None, out_dtype=bf16,
                 tm=1024, tn=512, blk_e=None, n_used=None, n_outer=False):
    m, k = x.shape
    n = wa.shape[-1]
    cast_w = wa.dtype == f32
    assert n_outer or not cast_w
    tm, tn, grid, ij, blk_e, n_used = _mm_setup(m, n, tm, tn, blk_e, n_used, n_outer)
    in_specs = [pl.BlockSpec((tm, k), ij(lambda i, j, be, nu: (i, 0))),
                pl.BlockSpec((None, k, tn), ij(_wmap)),
                pl.BlockSpec((None, k, tn), ij(_wmap))]
    args = [x, wa, wb]
    if ba is not None:
        in_specs += [pl.BlockSpec((1, tn), ij(lambda i, j, be, nu: (0, j)))] * 2
        args += [ba, bb]
    kern = functools.partial(_mm2_kernel, has_bias=ba is not None, act_a=act_a, act_b=act_b,
                             n_outer=n_outer, cast_w=cast_w)
    return pl.pallas_call(
        kern,
        grid_spec=pltpu.PrefetchScalarGridSpec(
            num_scalar_prefetch=2, grid=grid, in_specs=in_specs,
            out_specs=pl.BlockSpec((tm, tn), ij(lambda i, j, be, nu: (i, j))),
            scratch_shapes=[pltpu.VMEM((k, tn), bf16)] * 2 if cast_w else []),
        out_shape=jax.ShapeDtypeStruct((m, n), out_dtype),
        compiler_params=_cparams("parallel", "arbitrary"),
        name="matmul_gated",
    )(blk_e, n_used, *args)


def _gelu_tanh(x):
    c = np.sqrt(2.0 / np.pi).astype(np.float32)
    return 0.5 * x * (1.0 + jnp.tanh(c * (x + 0.044715 * (x * x * x))))


def _compress_kernel(x_ref, pos_ref, w1_ref, w2_ref, o_ref, *, n_c):
    half = CMP_STRIDE * NSA_HEAD_DIM
    x = x_ref[...].astype(f32)
    rows = x.shape[0]
    xn = pltpu.roll(x, rows - 1, 0)
    pos = pos_ref[...]
    xa = (x + pos[:, :half]).astype(bf16)
    xb = (xn + pos[:, half:]).astype(bf16)
    h = _dot(xa, w1_ref[:half, :]) + _dot(xb, w1_ref[half:, :])
    o = _dot(_gelu_tanh(h).astype(bf16), w2_ref[...])
    ridx = lax.broadcasted_iota(jnp.int32, o.shape, 0)
    o_ref[...] = jnp.where(ridx < n_c, o, 0.0).astype(o_ref.dtype)


def nsa_compress(xg, pos, w1, w2, n_c):
    b, _, g, rows, width = xg.shape
    kern = functools.partial(_compress_kernel, n_c=n_c)
    return pl.pallas_call(
        kern,
        grid=(b, 2, g),
        in_specs=[pl.BlockSpec((None, None, None, rows, width), lambda i, s, j: (i, s, j, 0, 0)),
                  pl.BlockSpec((None, 1, 2 * width), lambda i, s, j: (s, 0, 0)),
                  pl.BlockSpec((None, 2 * width, CMP_HIDDEN), lambda i, s, j: (s, 0, 0)),
                  pl.BlockSpec((None, CMP_HIDDEN, NSA_HEAD_DIM), lambda i, s, j: (s, 0, 0))],
        out_specs=pl.BlockSpec((None, None, None, rows, NSA_HEAD_DIM),
                               lambda i, s, j: (i, s, j, 0, 0)),
        out_shape=jax.ShapeDtypeStruct((b, 2, g, rows, NSA_HEAD_DIM), bf16),
        compiler_params=_cparams("parallel", "parallel", "parallel"),
        name="nsa_compress",
    )(xg, pos, w1, w2)


def _head_slopes(rows_head, g):
    h = g * NSA_REP + rows_head
    out = jnp.zeros(h.shape, f32)
    for hh in range(NSA_HEADS):
        out = jnp.where(h == hh, np.float32(2.0 ** (-8.0 * (hh + 1) / NSA_HEADS)), out)
    return out


def _cmp_kernel(q_ref, kc_ref, vc_ref, ov_ref, o_ref, selt_ref, *, n_c, n_s, tq):
    g = pl.program_id(1)
    q0 = pl.program_id(2) * tq
    scale = np.float32(NSA_HEAD_DIM ** -0.5)
    tpos = q0 + lax.broadcasted_iota(jnp.int32, (tq, 1), 0)
    lane = lax.broadcasted_iota(jnp.int32, (1, LANES), 1)
    c_dist = (tpos - (lane * CMP_STRIDE + (CMP_BLOCK - 1))).astype(f32)
    cmask = (c_dist >= 0) & (lane < n_c)
    kc = kc_ref[...]
    vc = vc_ref[...]
    psum = jnp.zeros((tq, LANES), f32)
    for r in range(NSA_REP):
        slope = _head_slopes(jnp.full((1, 1), r, jnp.int32), g)
        qr = q_ref[:, r * NSA_HEAD_DIM:(r + 1) * NSA_HEAD_DIM]
        s = _dot_nt(qr, kc) * scale - slope * c_dist
        s = jnp.where(cmask, s, NEG_BIG)
        m = jnp.max(s, axis=-1, keepdims=True)
        e = jnp.where(cmask, jnp.exp(s - m), 0.0)
        p = e / jnp.maximum(jnp.sum(e, axis=-1, keepdims=True), TINY)
        o_ref[:, r * NSA_HEAD_DIM:(r + 1) * NSA_HEAD_DIM] = _dot(p.astype(bf16), vc).astype(o_ref.dtype)
        psum = psum + p
    hi, mid, lo = _split3(psum)
    ov = ov_ref[...]
    imp = _dot(hi, ov) + _dot(mid, ov) + _dot(lo, ov)
    cur = tpos // SEL_BLOCK
    valid = lane * SEL_BLOCK <= tpos
    forced = valid & ((lane == 0) | (lane == cur) | (lane == cur - 1))
    score = jnp.where(valid, imp, -1.0) + jnp.where(forced, FORCE_BONUS, 0.0)
    score = jnp.where(lane < n_s, score, LOWEST)
    lane_f = lane.astype(f32)
    sel = jnp.zeros((tq, LANES), f32)
    for _ in range(min(SEL_TOPN, n_s)):
        m = jnp.max(score, axis=-1, keepdims=True)
        first = jnp.min(jnp.where(score == m, lane_f, 1e9), axis=-1, keepdims=True)
        hit = lane_f == first
        sel = jnp.where(hit, 1.0, sel)
        score = jnp.where(hit, LOWEST, score)
    selt_ref[...] = sel.T[:selt_ref.shape[0], :]


def nsa_cmp_attention(q, kvc, overlap, n_c, n_s, tq=ROW_TILE):
    b, t, _ = q.shape
    g = NSA_KV_GROUPS
    ncp = kvc.shape[3]
    assert ncp == LANES, "compressed blocks are laid out on one lane tile"
    assert n_s % SUBLANES == 0 and n_s <= LANES
    gw = NSA_REP * NSA_HEAD_DIM
    kern = functools.partial(_cmp_kernel, n_c=n_c, n_s=n_s, tq=tq)
    return pl.pallas_call(
        kern,
        grid=(b, g, t // tq),
        in_specs=[pl.BlockSpec((None, tq, gw), lambda i, j, k: (i, k, j)),
                  pl.BlockSpec((None, None, None, ncp, NSA_HEAD_DIM), lambda i, j, k: (i, 0, j, 0, 0)),
                  pl.BlockSpec((None, None, None, ncp, NSA_HEAD_DIM), lambda i, j, k: (i, 1, j, 0, 0)),
                  pl.BlockSpec((LANES, LANES), lambda i, j, k: (0, 0))],
        out_specs=[pl.BlockSpec((None, tq, gw), lambda i, j, k: (i, k, j)),
                   pl.BlockSpec((None, None, n_s, tq), lambda i, j, k: (i, j, 0, k))],
        out_shape=[jax.ShapeDtypeStruct((b, t, g * gw), bf16),
                   jax.ShapeDtypeStruct((b, g, n_s, t), f32)],
        compiler_params=_cparams("parallel", "parallel", "parallel"),
        name="nsa_cmp",
    )(q, kvc, kvc, overlap)


def _selwin_kernel(q_ref, ks_ref, vs_ref, kw_ref, vw_ref, selt_ref, oc_ref, gt_ref, o_ref, *, tq, tk):
    g = pl.program_id(1)
    q0 = pl.program_id(2) * tq
    rep = NSA_REP
    dh = NSA_HEAD_DIM
    nq = rep * tq
    span = WINDOW + tq
    c_qk = LOG2E * dh ** -0.5
    q4 = jnp.concatenate([q_ref[:, r * dh:(r + 1) * dh] for r in range(rep)], axis=0)
    nslope = [-LOG2E * _head_slopes(jnp.full((1, 1), r, jnp.int32), g) for r in range(rep)]

    def scores(k, dist, ok):
        bias = jnp.concatenate([jnp.where(ok, nslope[r] * dist, NEG_BIG) for r in range(rep)], axis=1)
        return _dot_nt(k, q4) * c_qk + bias

    def rel_pos(rows):
        return (q0 + lax.broadcasted_iota(jnp.int32, (rows, tq), 1)
                - lax.broadcasted_iota(jnp.int32, (rows, tq), 0))

    d_sel = rel_pos(tk)
    per_tile = tk // SEL_BLOCK

    def sel_body(kt, carry):
        m, l, acc = carry
        k0 = pl.multiple_of(kt * tk, tk)
        dist = (d_sel - k0).astype(f32)
        flags = selt_ref[pl.ds(pl.multiple_of(kt * per_tile, per_tile), per_tile), :]
        picked = jnp.concatenate([jnp.broadcast_to(flags[i:i + 1, :], (SEL_BLOCK, tq))
                                  for i in range(per_tile)], axis=0)
        ok = jnp.where(dist >= 0, picked, 0.0) > 0.5
        s = scores(ks_ref[pl.ds(k0, tk), :], dist, ok)
        m_new = jnp.maximum(m, jnp.max(s, axis=0, keepdims=True))
        alpha = jnp.exp2(m - m_new)
        p = jnp.exp2(s - m_new)
        l = alpha * l + jnp.sum(p, axis=0, keepdims=True)
        acc = alpha * acc + _dot_tn(vs_ref[pl.ds(k0, tk), :], p.astype(bf16))
        return m_new, l, acc

    init = (jnp.full((1, nq), NEG_BIG, f32), jnp.zeros((1, nq), f32), jnp.zeros((dh, nq), f32))
    _, l, acc = lax.fori_loop(0, (q0 + tq + tk - 1) // tk, sel_body, init)
    ot_sel = acc / jnp.maximum(l, TINY)

    k_lo = pl.multiple_of(jnp.maximum(q0 - WINDOW, 0), tq)
    dist = (rel_pos(span) - k_lo).astype(f32)
    ok = jnp.abs(dist - 0.5 * (WINDOW - 1)) <= 0.5 * (WINDOW - 1)
    s = scores(kw_ref[pl.ds(k_lo, span), :], dist, ok)
    m = jnp.max(s, axis=0, keepdims=True)
    p = jnp.exp2(s - m)
    l = jnp.sum(p, axis=0, keepdims=True)
    ot_win = _dot_tn(vw_ref[pl.ds(k_lo, span), :], p.astype(bf16)) / jnp.maximum(l, TINY)

    gate = _sigmoid(gt_ref[...])
    for r in range(rep):
        oc = oc_ref[:, r * dh:(r + 1) * dh].astype(f32)
        os_ = ot_sel[:, r * tq:(r + 1) * tq].T
        ow = ot_win[:, r * tq:(r + 1) * tq].T
        o = (gate[:, r:r + 1] * oc + gate[:, rep + r:rep + r + 1] * os_
             + gate[:, 2 * rep + r:2 * rep + r + 1] * ow)
        o_ref[:, r * dh:(r + 1) * dh] = o.astype(o_ref.dtype)


def nsa_selwin_attention(qkv, sel, o_cmp, gates, tq=ATT_TQ, tk=ATT_TK):
    b, t, _ = qkv.shape
    g = NSA_KV_GROUPS
    gw = NSA_REP * NSA_HEAD_DIM
    n_s = sel.shape[2]
    assert t % tk == 0 and t % tq == 0 and WINDOW % tq == 0 and t >= WINDOW + tq
    assert tk % (SEL_BLOCK * SUBLANES) == 0 and n_s * SEL_BLOCK == t

    def kvspec(branch):
        return pl.BlockSpec((None, t, NSA_HEAD_DIM),
                            lambda i, j, k: (i, 0, NSA_HEADS + branch * g + j))

    kern = functools.partial(_selwin_kernel, tq=tq, tk=tk)
    return pl.pallas_call(
        kern,
        grid=(b, g, t // tq),
        in_specs=[pl.BlockSpec((None, tq, gw), lambda i, j, k: (i, k, j)),
                  kvspec(2), kvspec(3), kvspec(4), kvspec(5),
                  pl.BlockSpec((None, None, n_s, tq), lambda i, j, k: (i, j, 0, k)),
                  pl.BlockSpec((None, tq, gw), lambda i, j, k: (i, k, j)),
                  pl.BlockSpec((None, tq, LANES), lambda i, j, k: (i, k, j))],
        out_specs=pl.BlockSpec((None, tq, gw), lambda i, j, k: (i, k, j)),
        out_shape=jax.ShapeDtypeStruct(o_cmp.shape, bf16),
        compiler_params=_cparams("parallel", "parallel", "parallel"),
        name="nsa_selwin",
    )(qkv, qkv, qkv, qkv, qkv, sel, o_cmp, gates)


def _hgrn_tables(c):
    levels = []
    m = c // 2
    while m >= 1:
        levels.append(m)
        m //= 2
    t = np.arange(c)[:, None]
    u = np.arange(c)[None, :]
    mats = [(u <= t)]
    masks = []
    for li, m in enumerate(levels):
        ref = (t // (2 * m)) * 2 * m + m - 1
        right = (t % (2 * m)) >= m
        if li >= HGRN_BCAST_LEVELS:
            mats.append(u <= ref)
        s = np.arange(c)[None, :]
        masks.append((t // (2 * m) == s // (2 * m)) & right & ((s % (2 * m)) < m))
    lstack = np.concatenate([x.astype(np.float32) for x in mats], axis=0)
    lstack = np.concatenate([lstack, lstack], axis=1)
    return levels, lstack, np.stack([x.astype(np.float32) for x in masks])


def _hgrn_kernel(q_ref, f_ref, i_ref, g_ref, la_ref, lc_ref, oml_ref, ng_ref, ls_ref, mk_ref,
                 o_ref, expo_ref, st_ref, *, c, n_levels):
    d = HGRN_DIM

    @pl.when(pl.program_id(1) == 0)
    def _():
        st_ref[...] = jnp.zeros(st_ref.shape, f32)

    z = f_ref[...]
    log_sig = jnp.minimum(z, 0.0) - jnp.log(1.0 + jnp.exp(-jnp.abs(z)))
    y = lc_ref[...] + log_sig
    a = la_ref[...]
    log_f = jnp.maximum(a, y) + jnp.log(1.0 + jnp.exp(-jnp.abs(a - y)))
    kk = oml_ref[...] * (1.0 / (1.0 + jnp.exp(z)))
    qf = _silu(q_ref[...])
    v = i_ref[...]
    gt = _silu(g_ref[...])
    lf2 = log_f * LOG2E
    hi = lf2.astype(bf16)
    lo = (lf2 - hi.astype(f32)).astype(bf16)
    expo_ref[...] = _dot(ls_ref[...], jnp.concatenate([hi, lo], axis=0))
    ng = ng_ref[...]
    row = lax.broadcasted_iota(jnp.int32, (c, d), 0)
    for h in range(HGRN_HEADS):
        sl = slice(h * d, (h + 1) * d)
        b = expo_ref[0:c, sl]
        qh = qf[:, sl]
        kh = kk[:, sl]
        vh = v[:, sl]
        vb = vh.astype(bf16)
        st = st_ref[h]
        o = _dot_nt((qh * jnp.exp2(b)).astype(bf16), st.astype(bf16))
        amat = jnp.zeros((c, c), f32)
        for li in range(n_levels):
            m = c >> (li + 1)
            if li < HGRN_BCAST_LEVELS:
                bref = b[m - 1:m, :]
                for blk in range(1, 1 << li):
                    bref = jnp.where(row >= 2 * m * blk, b[2 * m * blk + m - 1:2 * m * blk + m, :], bref)
            else:
                k0 = (1 + li - HGRN_BCAST_LEVELS) * c
                bref = expo_ref[k0:k0 + c, sl]
            e = jnp.exp2(-jnp.abs(b - bref))
            part = _dot_nt((qh * e).astype(bf16), (kh * e).astype(bf16))
            amat = amat + mk_ref[li] * part
        diag = jnp.sum(qh * kh, axis=-1, keepdims=True)
        o = o + _dot(amat.astype(bf16), vb) + diag * vh
        b_last = b[c - 1:c, :]
        st_ref[h] = st * jnp.exp2(b_last) + _dot_tn(vb, (kh * jnp.exp2(b_last - b)).astype(bf16))
        o = o * lax.rsqrt(jnp.mean(o * o, axis=-1, keepdims=True) + EPS) * ng
        o_ref[:, sl] = (o * gt[:, sl]).astype(o_ref.dtype)


def hgrn2(proj, col0, log_lb, log1m_lb, one_m_lb, norm_g, c=HGRN_CHUNK):
    b, t, _ = proj.shape
    hd = HGRN_HEADS * HGRN_DIM
    levels, lstack, masks = _hgrn_tables(c)
    cb = col0 // hd
    assert col0 % hd == 0 and t % c == 0
    nl = len(levels)
    kern = functools.partial(_hgrn_kernel, c=c, n_levels=nl)

    def seg(k):
        return pl.BlockSpec((None, c, hd), lambda i, j: (i, j, cb + k))

    def row(width):
        return pl.BlockSpec((1, width), lambda i, j: (0, 0))

    return pl.pallas_call(
        kern,
        grid=(b, t // c),
        in_specs=[seg(0), seg(1), seg(2), seg(3), row(hd), row(hd), row(hd), row(HGRN_DIM),
                  pl.BlockSpec(lstack.shape, lambda i, j: (0, 0)),
                  pl.BlockSpec(masks.shape, lambda i, j: (0, 0, 0))],
        out_specs=pl.BlockSpec((None, c, hd), lambda i, j: (i, j, 0)),
        out_shape=jax.ShapeDtypeStruct((b, t, hd), bf16),
        scratch_shapes=[pltpu.VMEM((lstack.shape[0], hd), f32),
                        pltpu.VMEM((HGRN_HEADS, HGRN_DIM, HGRN_DIM), f32)],
        compiler_params=_cparams("parallel", "arbitrary"),
        name="hgrn2",
    )(proj, proj, proj, proj, log_lb, log1m_lb, one_m_lb, norm_g,
      jnp.asarray(lstack, bf16), jnp.asarray(masks, f32))


def _conv_kernel(x_ref, w_ref, b_ref, g_ref, bb_ref, o_ref, buf_ref, acc_ref, *, tt, sub):
    halo = CONV_HALO
    ch = x_ref.shape[-1]

    @pl.when(pl.program_id(1) == 0)
    def _():
        buf_ref[0:halo, :] = jnp.zeros((halo, ch), f32)

    buf_ref[halo:halo + tt, :] = x_ref[...]
    base = halo - (CONV_WIDTH - 1)

    def lane_block(cb, carry):
        c0 = pl.multiple_of(cb * LANES, LANES)
        w = w_ref[:, pl.ds(c0, LANES)]
        for ts in range(tt // sub):
            acc = jnp.zeros((sub, LANES), f32) + b_ref[:, pl.ds(c0, LANES)]
            for phase in range(SUBLANES):
                taps = [j for j in range(CONV_WIDTH) if (base + j) % SUBLANES == phase]
                reach = max((base + j) // SUBLANES for j in taps) * SUBLANES
                if phase == 0:
                    slab = buf_ref[pl.ds(ts * sub, sub + reach), pl.ds(c0, LANES)]
                else:
                    rows = sub + reach + SUBLANES
                    slab = pltpu.roll(buf_ref[pl.ds(ts * sub, rows), pl.ds(c0, LANES)], rows - phase, 0)
                for j in taps:
                    off = (base + j) // SUBLANES * SUBLANES
                    acc = acc + w[j:j + 1, :] * slab[off:off + sub, :]
            acc_ref[ts * sub:(ts + 1) * sub, pl.ds(c0, LANES)] = acc
        return carry

    lax.fori_loop(0, ch // LANES, lane_block, 0)
    buf_ref[0:halo, :] = buf_ref[tt:tt + halo, :]
    y = acc_ref[...]
    mu = jnp.mean(y, axis=-1, keepdims=True)
    yc = y - mu
    var = jnp.mean(yc * yc, axis=-1, keepdims=True)
    yn = yc * lax.rsqrt(var + EPS) * g_ref[...] + bb_ref[...]
    o_ref[...] = _silu(yn).astype(o_ref.dtype)


def conv_ln_swish(x, w_dw, b_dw, ln_g, ln_b, tt=CONV_TT, sub=64):
    b, t, ch = x.shape
    tt = min(tt, t)
    assert t % tt == 0 and tt % sub == 0 and tt >= CONV_HALO
    wpad = jnp.zeros((CONV_HALO, ch), f32).at[:CONV_WIDTH].set(w_dw)
    kern = functools.partial(_conv_kernel, tt=tt, sub=sub)

    def row():
        return pl.BlockSpec((1, ch), lambda i, j: (0, 0))

    return pl.pallas_call(
        kern,
        grid=(b, t // tt),
        in_specs=[pl.BlockSpec((None, tt, ch), lambda i, j: (i, j, 0)),
                  pl.BlockSpec((CONV_HALO, ch), lambda i, j: (0, 0)), row(), row(), row()],
        out_specs=pl.BlockSpec((None, tt, ch), lambda i, j: (i, j, 0)),
        out_shape=jax.ShapeDtypeStruct((b, t, ch), bf16),
        scratch_shapes=[pltpu.VMEM((tt + CONV_HALO, ch), f32), pltpu.VMEM((tt, ch), f32)],
        compiler_params=_cparams("parallel", "arbitrary"),
        name="conv_ln_swish",
    )(x, wpad, b_dw.reshape(1, ch), ln_g.reshape(1, ch), ln_b.reshape(1, ch))


def _router_kernel(x_ref, g_ref, w_ref, tri_ref, u_ref, meta_ref, cnt_ref, run_ref):
    @pl.when(pl.program_id(0) == 0)
    def _():
        run_ref[...] = jnp.zeros(run_ref.shape, f32)

    x = x_ref[...]
    u = x * lax.rsqrt(jnp.mean(x * x, axis=-1, keepdims=True) + EPS) * g_ref[...]
    u_ref[...] = u.astype(u_ref.dtype)
    uh, um, _ = _split3(u)
    wh, wm, _ = _split3(w_ref[...])
    logits = _dot(uh, wh) + (_dot(uh, wm) + _dot(um, wh))
    lane = lax.broadcasted_iota(jnp.int32, (1, LANES), 1)
    lane_f = lane.astype(f32)
    l1 = jnp.where(lane < N_EXPERTS, logits, LOWEST)
    m1 = jnp.max(l1, axis=-1, keepdims=True)
    i1 = jnp.min(jnp.where(l1 == m1, lane_f, 1e9), axis=-1, keepdims=True)
    l2 = jnp.where(lane_f == i1, LOWEST, l1)
    m2 = jnp.max(l2, axis=-1, keepdims=True)
    i2 = jnp.min(jnp.where(l2 == m2, lane_f, 1e9), axis=-1, keepdims=True)
    e2 = jnp.exp(m2 - m1)
    g1 = 1.0 / (1.0 + e2)
    g2 = e2 / (1.0 + e2)
    hit1 = lane_f == i1
    hit2 = lane_f == i2
    onehot = jnp.where(hit1 | hit2, 1.0, 0.0)
    cum = _dot(tri_ref[...], onehot.astype(bf16)) + run_ref[...]
    r1 = jnp.sum(jnp.where(hit1, cum, 0.0), axis=-1, keepdims=True)
    r2 = jnp.sum(jnp.where(hit2, cum, 0.0), axis=-1, keepdims=True)
    run = run_ref[...] + jnp.sum(onehot, axis=0, keepdims=True)
    run_ref[...] = run
    cnt_ref[...] = run
    meta = jnp.zeros(meta_ref.shape, f32)
    for col, val in enumerate((i1, i2, g1, g2, r1, r2)):
        meta = jnp.where(lane == col, val, meta)
    meta_ref[...] = meta


def moe_router(h, norm_g, w_router, tm=ROUTER_TM):
    m, d = h.shape
    tm = min(tm, m)
    wpad = jnp.zeros((d, LANES), f32).at[:, :N_EXPERTS].set(w_router)
    tri = jnp.asarray(np.tril(np.ones((tm, tm), np.float32), -1), bf16)
    return pl.pallas_call(
        _router_kernel,
        grid=(m // tm,),
        in_specs=[pl.BlockSpec((tm, d), lambda i: (i, 0)),
                  pl.BlockSpec((1, d), lambda i: (0, 0)),
                  pl.BlockSpec((d, LANES), lambda i: (0, 0)),
                  pl.BlockSpec((tm, tm), lambda i: (0, 0))],
        out_specs=[pl.BlockSpec((tm, d), lambda i: (i, 0)),
                   pl.BlockSpec((tm, LANES), lambda i: (i, 0)),
                   pl.BlockSpec((1, LANES), lambda i: (0, 0))],
        out_shape=[jax.ShapeDtypeStruct((m, d), bf16),
                   jax.ShapeDtypeStruct((m, LANES), f32),
                   jax.ShapeDtypeStruct((1, LANES), f32)],
        scratch_shapes=[pltpu.VMEM((1, LANES), f32)],
        compiler_params=_cparams("arbitrary"),
        name="moe_router",
    )(h, norm_g.reshape(1, d), wpad, tri)


def _dispatch_kernel(dest_ref, u_ref, xs_in_ref, xs_ref, sem, *, tm):
    del xs_in_ref

    def issue(r, carry):
        for k in range(2):
            pltpu.make_async_copy(u_ref.at[r], xs_ref.at[dest_ref[0, 2 * r + k]], sem).start()
        return carry

    lax.fori_loop(0, tm, issue, 0)

    def drain(r, carry):
        for k in range(2):
            pltpu.make_async_copy(u_ref.at[0], xs_ref.at[0], sem).wait()
        return carry

    lax.fori_loop(0, tm, drain, 0)


def moe_dispatch(u3, dest, p_rows, tm=ROW_TILE):
    n, s, _ = u3.shape
    tm = min(tm, n)
    xs0 = jnp.zeros((p_rows, s, LANES), u3.dtype)
    kern = functools.partial(_dispatch_kernel, tm=tm)
    return pl.pallas_call(
        kern,
        grid=(n // tm,),
        in_specs=[pl.BlockSpec((None, 1, 2 * tm), lambda i: (i, 0, 0), memory_space=pltpu.SMEM),
                  pl.BlockSpec((tm, s, LANES), lambda i: (i, 0, 0)),
                  pl.BlockSpec(memory_space=pl.ANY)],
        out_specs=pl.BlockSpec(memory_space=pl.ANY),
        out_shape=jax.ShapeDtypeStruct(xs0.shape, xs0.dtype),
        scratch_shapes=[pltpu.SemaphoreType.DMA(())],
        input_output_aliases={2: 0},
        compiler_params=_cparams("arbitrary"),
        name="moe_dispatch",
    )(dest.reshape(n // tm, 1, 2 * tm), u3, xs0)


def _combine_kernel(dest_ref, gate_ref, y_ref, h_ref, o_ref, buf_ref, sem, *, tm):
    def issue(r, carry):
        for k in range(2):
            pltpu.make_async_copy(y_ref.at[dest_ref[0, 2 * r + k]], buf_ref.at[k, r], sem).start()
        return carry

    lax.fori_loop(0, tm, issue, 0)

    def drain(r, carry):
        for k in range(2):
            pltpu.make_async_copy(y_ref.at[0], buf_ref.at[0, 0], sem).wait()
        return carry

    lax.fori_loop(0, tm, drain, 0)

    def mix(r, carry):
        o_ref[r] = (h_ref[r] + gate_ref[0, 2 * r] * buf_ref[0, r]
                    + gate_ref[0, 2 * r + 1] * buf_ref[1, r])
        return carry

    lax.fori_loop(0, tm, mix, 0)


def moe_combine(y3, dest, gate, h3, tm=ROW_TILE):
    n, s, _ = h3.shape
    tm = min(tm, n)
    kern = functools.partial(_combine_kernel, tm=tm)
    return pl.pallas_call(
        kern,
        grid=(n // tm,),
        in_specs=[pl.BlockSpec((None, 1, 2 * tm), lambda i: (i, 0, 0), memory_space=pltpu.SMEM),
                  pl.BlockSpec((None, 1, 2 * tm), lambda i: (i, 0, 0), memory_space=pltpu.SMEM),
                  pl.BlockSpec(memory_space=pl.ANY),
                  pl.BlockSpec((tm, s, LANES), lambda i: (i, 0, 0))],
        out_specs=pl.BlockSpec((tm, s, LANES), lambda i: (i, 0, 0)),
        out_shape=jax.ShapeDtypeStruct(h3.shape, f32),
        scratch_shapes=[pltpu.VMEM((2, tm, s, LANES), f32), pltpu.SemaphoreType.DMA(())],
        compiler_params=_cparams("arbitrary"),
        name="moe_combine",
    )(dest.reshape(n // tm, 1, 2 * tm), gate.reshape(n // tm, 1, 2 * tm), y3, h3)


def _mixer_ab(h2, bsz, t, norm_g, w_in, w_out, pos_k, pos_v, w1_k, w1_v, w2_k, w2_v, lb, hgrn_norm):
    m, d = h2.shape
    nq = NSA_HEADS * NSA_HEAD_DIM
    nkv = NSA_KV_GROUPS * NSA_HEAD_DIM
    hd = HGRN_HEADS * HGRN_DIM
    g, rep = NSA_KV_GROUPS, NSA_REP
    u = rmsnorm(h2, norm_g, bf16)
    o_kv = nq
    o_gate = o_kv + 6 * nkv
    o_hg = o_gate + 3 * NSA_HEADS
    w_attn = w_in[:, :o_gate].astype(bf16)[None]
    w_hg = w_in[:, o_hg:o_hg + 4 * hd].astype(bf16)[None]
    wg = w_in[:, o_gate:o_hg].reshape(d, g, rep, 3).transpose(0, 1, 3, 2).reshape(d, g, 3 * rep)
    wg = jnp.pad(wg, ((0, 0), (0, 0), (0, LANES - 3 * rep))).reshape(d, g * LANES).astype(bf16)[None]

    attn = matmul(u, w_attn, out_dtype=bf16)
    hg = matmul(u, w_hg, out_dtype=f32, tn=1024)
    gates = matmul(u, wg, out_dtype=f32, tn=g * LANES)

    qkv = attn.reshape(bsz, t, nq + 6 * nkv)
    n_c = (t - CMP_BLOCK) // CMP_STRIDE + 1
    n_s = t // SEL_BLOCK
    rows = t // CMP_STRIDE
    xg = qkv[:, :, nq:nq + 2 * nkv].reshape(bsz, t, 2, g, NSA_HEAD_DIM).transpose(0, 2, 3, 1, 4)
    xg = xg.reshape(bsz, 2, g, rows, CMP_STRIDE * NSA_HEAD_DIM)
    pos = jnp.stack([pos_k, pos_v]).reshape(2, 1, CMP_BLOCK * NSA_HEAD_DIM)
    kvc = nsa_compress(xg, pos, jnp.stack([w1_k, w1_v]).astype(bf16),
                       jnp.stack([w2_k, w2_v]).astype(bf16), n_c)
    cj = np.arange(LANES)[:, None] * CMP_STRIDE
    sk = np.arange(LANES)[None, :] * SEL_BLOCK
    overlap = ((cj < sk + SEL_BLOCK) & (cj + CMP_BLOCK > sk)
               & (np.arange(LANES)[:, None] < n_c) & (np.arange(LANES)[None, :] < n_s))
    o_cmp, sel = nsa_cmp_attention(qkv, kvc, jnp.asarray(overlap.astype(np.float32), bf16), n_c, n_s)
    o_a = nsa_selwin_attention(qkv, sel, o_cmp, gates.reshape(bsz, t, g * LANES))

    lbc = jnp.clip(lb.astype(f32), 0.0, LB_MAX).reshape(1, hd)
    o_b = hgrn2(hg.reshape(bsz, t, 4 * hd), 0, jnp.log(jnp.maximum(lbc, TINY)), jnp.log1p(-lbc),
                1.0 - lbc, hgrn_norm.reshape(1, HGRN_DIM))
    mix = jnp.concatenate([o_a.reshape(m, nq), o_b.reshape(m, hd)], axis=-1)
    return matmul(mix, w_out.astype(bf16)[None], residual=h2, out_dtype=f32, tn=1024)


def _dense_ffn(h2, norm_g, w1_all, w3_all, w2, j, tm=1024):
    u = rmsnorm(h2, norm_g, bf16)
    tm = _largest_tile(h2.shape[0], tm, SUBLANES)
    nblk = h2.shape[0] // tm
    mid = matmul_gated(u, w1_all, w3_all, act_a="silu", tm=tm, n_outer=True,
                       blk_e=jnp.full((nblk,), j, jnp.int32), n_used=jnp.full((1,), nblk, jnp.int32))
    return matmul(mid, w2.astype(bf16)[None], residual=h2, out_dtype=f32)


def _conformer(h2, bsz, t, norm_g, w_pw1, b_pw1, w_dw, b_dw, ln_g, ln_b, w_pw2, b_pw2):
    m, d = h2.shape
    ch = w_dw.shape[-1]
    u = rmsnorm(h2, norm_g, bf16)
    glu = matmul_gated(u, w_pw1[:, :ch].astype(bf16)[None], w_pw1[:, ch:].astype(bf16)[None],
                       ba=b_pw1[:ch].reshape(1, ch), bb=b_pw1[ch:].reshape(1, ch),
                       act_b="sigmoid", out_dtype=f32, tn=1024)
    y = conv_ln_swish(glu.reshape(bsz, t, ch), w_dw, b_dw, ln_g, ln_b)
    return matmul(y.reshape(m, ch), w_pw2.astype(bf16)[None], bias=b_pw2.reshape(1, d),
                  residual=h2, out_dtype=f32, tn=1024)


def _moe(h2, norm_g, w_router, w1_all, w3_all, w2_all, j, tm=MOE_TM):
    n, d = h2.shape
    e = N_EXPERTS
    s = d // LANES
    u, meta, counts = moe_router(h2, norm_g, w_router)
    tm = min(tm, n)
    idx = meta[:, 0:2].astype(jnp.int32)
    gate = meta[:, 2:4]
    rank = meta[:, 4:6].astype(jnp.int32)
    cnt = counts[0, :e].astype(jnp.int32)
    nblk = (cnt + tm - 1) // tm
    blk_end = jnp.cumsum(nblk)
    pstart = (blk_end - nblk) * tm
    dest = pstart[idx] + rank
    total_blk = (2 * n) // tm + e
    p_rows = total_blk * tm
    blk_e = jnp.minimum(jnp.searchsorted(blk_end, jnp.arange(total_blk), side="right"), e - 1)
    blk_e = blk_e.astype(jnp.int32)
    n_used = blk_end[-1:].astype(jnp.int32)

    xs = moe_dispatch(u.reshape(n, s, LANES), dest, p_rows).reshape(p_rows, d)
    mid = matmul_gated(xs, w1_all, w3_all, act_a="silu", tm=tm, tn=1024, n_outer=True,
                       blk_e=blk_e + j * e, n_used=n_used)
    y = matmul(mid, w2_all, out_dtype=f32, tm=tm, blk_e=blk_e + j * e, n_used=n_used, n_outer=True)
    out = moe_combine(y.reshape(p_rows, s, LANES), dest, gate, h2.reshape(n, s, LANES))
    return out.reshape(n, d)


def _lower_bounds(table):
    p = jax.nn.softmax(table.astype(f32), axis=0)
    return jnp.cumsum(p, axis=0) - p[0]


def kernel(x, norm_mix, norm_ffn, final_norm, ab_w_in, ab_w_out, cmp_pos_k, cmp_pos_v, cmp_w1_k, cmp_w1_v, cmp_w2_k, cmp_w2_v, hgrn_lower_bounds, hgrn_norm, ffn_w1, ffn_w3, ffn_w2, conv_w_pw1, conv_b_pw1, conv_w_dw, conv_b_dw, conv_ln_g, conv_ln_b, conv_w_pw2, conv_b_pw2, moe_router, moe_w1, moe_w3, moe_w2):
    bsz, t, d = x.shape
    depth = norm_mix.shape[0]
    lbs = _lower_bounds(hgrn_lower_bounds)
    h = x.reshape(bsz * t, d)
    for layer in range(depth):
        j = layer // 2
        if layer % 2 == 0:
            h = _mixer_ab(h, bsz, t, norm_mix[layer], ab_w_in[j], ab_w_out[j], cmp_pos_k[j],
                          cmp_pos_v[j], cmp_w1_k[j], cmp_w1_v[j], cmp_w2_k[j], cmp_w2_v[j],
                          lbs[j], hgrn_norm[j])
            h = _dense_ffn(h, norm_ffn[layer], ffn_w1, ffn_w3, ffn_w2[j], j)
        else:
            h = _conformer(h, bsz, t, norm_mix[layer], conv_w_pw1[j], conv_b_pw1[j], conv_w_dw[j],
                           conv_b_dw[j], conv_ln_g[j], conv_ln_b[j], conv_w_pw2[j], conv_b_pw2[j])
            fe = moe_w1.shape[-1]
            h = _moe(h, norm_ffn[layer], moe_router[j], moe_w1.reshape(-1, d, fe),
                     moe_w3.reshape(-1, d, fe), moe_w2.reshape(-1, fe, d), j)
    return rmsnorm(h, final_norm, f32).reshape(bsz, t, d)
```

```python
import functools

import numpy as np
import jax
import jax.numpy as jnp
from jax import lax
from jax.experimental import pallas as pl
from jax.experimental.pallas import tpu as pltpu

f32 = jnp.float32
bf16 = jnp.bfloat16

NSA_HEADS = 8
NSA_KV_GROUPS = 2
NSA_HEAD_DIM = 128
NSA_REP = NSA_HEADS // NSA_KV_GROUPS
CMP_BLOCK = 32
CMP_STRIDE = 16
CMP_HIDDEN = 256
SEL_BLOCK = 64
SEL_TOPN = 8
WINDOW = 512
FORCE_BONUS = 1.0e4
HGRN_HEADS = 8
HGRN_DIM = 128
LB_MAX = 1.0 - 1e-6
CONV_WIDTH = 31
N_EXPERTS = 8
EPS = 1e-6
NEG_BIG = -1e30
TINY = 1e-30
LOWEST = -3.0e38
LOG2E = 1.4426950408889634

LANES = 128
SUBLANES = 8
VMEM_LIMIT = 56 * 1024 * 1024

HGRN_CHUNK = 128
HGRN_BCAST_LEVELS = 2
ATT_TQ = 128
ATT_TK = 512
CONV_TT = 256
CONV_HALO = 32
MOE_TM = 512
ROUTER_TM = 512
ROW_TILE = 256


def _cparams(*sem):
    return pltpu.CompilerParams(dimension_semantics=sem, vmem_limit_bytes=VMEM_LIMIT)


def _largest_tile(n, cap, unit):
    if n <= cap:
        return n
    best = max(k for k in range(unit, cap + 1, unit) if n % k == 0)
    return best


def _dot(a, b):
    return jnp.dot(a, b, preferred_element_type=f32)


def _dot_nt(a, b):
    return lax.dot_general(a, b, (((1,), (1,)), ((), ())), preferred_element_type=f32)


def _dot_tn(a, b):
    return lax.dot_general(a, b, (((0,), (0,)), ((), ())), preferred_element_type=f32)


def _sigmoid(x):
    return 1.0 / (1.0 + jnp.exp(-x))


def _silu(x):
    return x * _sigmoid(x)


def _split3(x):
    hi = x.astype(bf16)
    r = x - hi.astype(f32)
    mid = r.astype(bf16)
    lo = (r - mid.astype(f32)).astype(bf16)
    return hi, mid, lo


def _rmsnorm_kernel(x_ref, g_ref, o_ref):
    x = x_ref[...]
    y = x * lax.rsqrt(jnp.mean(x * x, axis=-1, keepdims=True) + EPS) * g_ref[...]
    o_ref[...] = y.astype(o_ref.dtype)


def rmsnorm(x, g, out_dtype, tm=1024):
    m, d = x.shape
    tm = _largest_tile(m, tm, SUBLANES)
    return pl.pallas_call(
        _rmsnorm_kernel,
        grid=(m // tm,),
        in_specs=[pl.BlockSpec((tm, d), lambda i: (i, 0)),
                  pl.BlockSpec((1, d), lambda i: (0, 0))],
        out_specs=pl.BlockSpec((tm, d), lambda i: (i, 0)),
        out_shape=jax.ShapeDtypeStruct((m, d), out_dtype),
        compiler_params=_cparams("parallel"),
        name="rmsnorm",
    )(x, g.reshape(1, d))


def _mm_kernel(be_ref, nu_ref, x_ref, w_ref, *rest, has_bias, has_res, n_outer, cast_w):
    rest = list(rest)
    b_ref = rest.pop(0) if has_bias else None
    r_ref = rest.pop(0) if has_res else None
    o_ref = rest.pop(0)
    i = pl.program_id(1 if n_outer else 0)
    if cast_w:
        w16_ref, = rest
        fresh = (i == 0) | (be_ref[i] != be_ref[jnp.maximum(i - 1, 0)])

        @pl.when((i < nu_ref[0]) & fresh)
        def _():
            w16_ref[...] = w_ref[...].astype(bf16)

        w_ref = w16_ref

    @pl.when(i < nu_ref[0])
    def _():
        acc = _dot(x_ref[...], w_ref[...])
        if has_bias:
            acc = acc + b_ref[...]
        if has_res:
            acc = acc + r_ref[...]
        o_ref[...] = acc.astype(o_ref.dtype)

    @pl.when(i >= nu_ref[0])
    def _():
        o_ref[...] = jnp.zeros(o_ref.shape, o_ref.dtype)


def _mm_setup(m, n, tm, tn, blk_e, n_used, n_outer):
    tm = _largest_tile(m, tm, SUBLANES)
    tn = _largest_tile(n, tn, LANES)
    nblk = m // tm
    if blk_e is None:
        blk_e = jnp.zeros((nblk,), jnp.int32)
        n_used = jnp.full((1,), nblk, jnp.int32)
    if n_outer:
        grid = (n // tn, nblk)

        def ij(f):
            return lambda j, i, be, nu: f(i, j, be, nu)
    else:
        grid = (nblk, n // tn)

        def ij(f):
            return f
    return tm, tn, grid, ij, blk_e, n_used


def _wmap(i, j, be, nu):
    return (be[i], 0, jnp.where(i < nu[0], j, 0))


def matmul(x, w, *, bias=None, residual=None, out_dtype=f32, tm=1024, tn=512,
           blk_e=None, n_used=None, n_outer=False):
    m, k = x.shape
    n = w.shape[-1]
    cast_w = w.dtype == f32
    assert n_outer or not cast_w
    tm, tn, grid, ij, blk_e, n_used = _mm_setup(m, n, tm, tn, blk_e, n_used, n_outer)
    w_mode = dict(pipeline_mode=pl.Buffered(1)) if cast_w else {}
    in_specs = [pl.BlockSpec((tm, k), ij(lambda i, j, be, nu: (i, 0))),
                pl.BlockSpec((None, k, tn), ij(_wmap), **w_mode)]
    args = [x, w]
    if bias is not None:
        in_specs.append(pl.BlockSpec((1, tn), ij(lambda i, j, be, nu: (0, j))))
        args.append(bias)
    if residual is not None:
        in_specs.append(pl.BlockSpec((tm, tn), ij(lambda i, j, be, nu: (i, j))))
        args.append(residual)
    kern = functools.partial(_mm_kernel, has_bias=bias is not None, has_res=residual is not None,
                             n_outer=n_outer, cast_w=cast_w)
    return pl.pallas_call(
        kern,
        grid_spec=pltpu.PrefetchScalarGridSpec(
            num_scalar_prefetch=2, grid=grid, in_specs=in_specs,
            out_specs=pl.BlockSpec((tm, tn), ij(lambda i, j, be, nu: (i, j))),
            scratch_shapes=[pltpu.VMEM((k, tn), bf16)] if cast_w else []),
        out_shape=jax.ShapeDtypeStruct((m, n), out_dtype),
        compiler_params=_cparams("parallel", "arbitrary"),
        name="matmul",
    )(blk_e, n_used, *args)


def _mm2_kernel(be_ref, nu_ref, x_ref, wa_ref, wb_ref, *rest, has_bias, act_a, act_b, n_outer, cast_w):
    rest = list(rest)
    ba_ref = rest.pop(0) if has_bias else None
    bb_ref = rest.pop(0) if has_bias else None
    o_ref = rest.pop(0)
    i = pl.program_id(1 if n_outer else 0)
    active = i < nu_ref[0]
    if cast_w:
        wa16_ref, wb16_ref = rest
        fresh = (i == 0) | (be_ref[i] != be_ref[jnp.maximum(i - 1, 0)])

        @pl.when(active & fresh)
        def _():
            wa16_ref[...] = wa_ref[...].astype(bf16)
            wb16_ref[...] = wb_ref[...].astype(bf16)

        wa_ref, wb_ref = wa16_ref, wb16_ref

    @pl.when(active)
    def _():
        x = x_ref[...]
        a = _dot(x, wa_ref[...])
        b = _dot(x, wb_ref[...])
        if has_bias:
            a = a + ba_ref[...]
            b = b + bb_ref[...]
        if act_a == "silu":
            a = _silu(a)
        if act_b == "sigmoid":
            b = _sigmoid(b)
        o_ref[...] = (a * b).astype(o_ref.dtype)

    @pl.when(jnp.logical_not(active))
    def _():
        o_ref[...] = jnp.zeros(o_ref.shape, o_ref.dtype)


def matmul_gated(x, wa, wb, *, ba=None, bb=None, act_a=None, act_b=None, out_dtype=bf16,
                 tm=1024, tn=512, blk_e=None, n_used=None, n_outer=False):
    m, k = x.shape
    n = wa.shape[-1]
    cast_w = wa.dtype == f32
    assert n_outer or not cast_w
    tm, tn, grid, ij, blk_e, n_used = _mm_setup(m, n, tm, tn, blk_e, n_used, n_outer)
    in_specs = [pl.BlockSpec((tm, k), ij(lambda i, j, be, nu: (i, 0))),
                pl.BlockSpec((None, k, tn), ij(_wmap)),
                pl.BlockSpec((None, k, tn), ij(_wmap))]
    args = [x, wa, wb]
    if ba is not None:
        in_specs += [pl.BlockSpec((1, tn), ij(lambda i, j, be, nu: (0, j)))] * 2
        args += [ba, bb]
    kern = functools.partial(_mm2_kernel, has_bias=ba is not None, act_a=act_a, act_b=act_b,
                             n_outer=n_outer, cast_w=cast_w)
    return pl.pallas_call(
        kern,
        grid_spec=pltpu.PrefetchScalarGridSpec(
            num_scalar_prefetch=2, grid=grid, in_specs=in_specs,
            out_specs=pl.BlockSpec((tm, tn), ij(lambda i, j, be, nu: (i, j))),
            scratch_shapes=[pltpu.VMEM((k, tn), bf16)] * 2 if cast_w else []),
        out_shape=jax.ShapeDtypeStruct((m, n), out_dtype),
        compiler_params=_cparams("parallel", "arbitrary"),
        name="matmul_gated",
    )(blk_e, n_used, *args)


def _gelu_tanh(x):
    c = np.sqrt(2.0 / np.pi).astype(np.float32)
    return 0.5 * x * (1.0 + jnp.tanh(c * (x + 0.044715 * (x * x * x))))


def _compress_kernel(x_ref, pos_ref, w1_ref, w2_ref, o_ref, *, n_c):
    half = CMP_STRIDE * NSA_HEAD_DIM
    x = x_ref[...].astype(f32)
    rows = x.shape[0]
    xn = pltpu.roll(x, rows - 1, 0)
    pos = pos_ref[...]
    xa = (x + pos[:, :half]).astype(bf16)
    xb = (xn + pos[:, half:]).astype(bf16)
    h = _dot(xa, w1_ref[:half, :]) + _dot(xb, w1_ref[half:, :])
    o = _dot(_gelu_tanh(h).astype(bf16), w2_ref[...])
    ridx = lax.broadcasted_iota(jnp.int32, o.shape, 0)
    o_ref[...] = jnp.where(ridx < n_c, o, 0.0).astype(o_ref.dtype)


def nsa_compress(xg, pos, w1, w2, n_c):
    b, _, g, rows, width = xg.shape
    kern = functools.partial(_compress_kernel, n_c=n_c)
    return pl.pallas_call(
        kern,
        grid=(b, 2, g),
        in_specs=[pl.BlockSpec((None, None, None, rows, width), lambda i, s, j: (i, s, j, 0, 0)),
                  pl.BlockSpec((None, 1, 2 * width), lambda i, s, j: (s, 0, 0)),
                  pl.BlockSpec((None, 2 * width, CMP_HIDDEN), lambda i, s, j: (s, 0, 0)),
                  pl.BlockSpec((None, CMP_HIDDEN, NSA_HEAD_DIM), lambda i, s, j: (s, 0, 0))],
        out_specs=pl.BlockSpec((None, None, None, rows, NSA_HEAD_DIM),
                               lambda i, s, j: (i, s, j, 0, 0)),
        out_shape=jax.ShapeDtypeStruct((b, 2, g, rows, NSA_HEAD_DIM), bf16),
        compiler_params=_cparams("parallel", "parallel", "parallel"),
        name="nsa_compress",
    )(xg, pos, w1, w2)


def _head_slopes(rows_head, g):
    h = g * NSA_REP + rows_head
    out = jnp.zeros(h.shape, f32)
    for hh in range(NSA_HEADS):
        out = jnp.where(h == hh, np.float32(2.0 ** (-8.0 * (hh + 1) / NSA_HEADS)), out)
    return out


def _cmp_kernel(q_ref, kc_ref, vc_ref, ov_ref, o_ref, selt_ref, *, n_c, n_s, tq):
    g = pl.program_id(1)
    q0 = pl.program_id(2) * tq
    scale = np.float32(NSA_HEAD_DIM ** -0.5)
    tpos = q0 + lax.broadcasted_iota(jnp.int32, (tq, 1), 0)
    lane = lax.broadcasted_iota(jnp.int32, (1, LANES), 1)
    c_dist = (tpos - (lane * CMP_STRIDE + (CMP_BLOCK - 1))).astype(f32)
    cmask = (c_dist >= 0) & (lane < n_c)
    kc = kc_ref[...]
    vc = vc_ref[...]
    psum = jnp.zeros((tq, LANES), f32)
    for r in range(NSA_REP):
        slope = _head_slopes(jnp.full((1, 1), r, jnp.int32), g)
        qr = q_ref[:, r * NSA_HEAD_DIM:(r + 1) * NSA_HEAD_DIM]
        s = _dot_nt(qr, kc) * scale - slope * c_dist
        s = jnp.where(cmask, s, NEG_BIG)
        m = jnp.max(s, axis=-1, keepdims=True)
        e = jnp.where(cmask, jnp.exp(s - m), 0.0)
        p = e / jnp.maximum(jnp.sum(e, axis=-1, keepdims=True), TINY)
        o_ref[:, r * NSA_HEAD_DIM:(r + 1) * NSA_HEAD_DIM] = _dot(p.astype(bf16), vc).astype(o_ref.dtype)
        psum = psum + p
    hi, mid, lo = _split3(psum)
    ov = ov_ref[...]
    imp = _dot(hi, ov) + _dot(mid, ov) + _dot(lo, ov)
    cur = tpos // SEL_BLOCK
    valid = lane * SEL_BLOCK <= tpos
    forced = valid & ((lane == 0) | (lane == cur) | (lane == cur - 1))
    score = jnp.where(valid, imp, -1.0) + jnp.where(forced, FORCE_BONUS, 0.0)
    score = score.T[:n_s, :]
    blk_f = lax.broadcasted_iota(jnp.int32, (n_s, tq), 0).astype(f32)
    sel = jnp.zeros((n_s, tq), f32)
    for _ in range(min(SEL_TOPN, n_s)):
        m = jnp.max(score, axis=0, keepdims=True)
        first = jnp.min(jnp.where(score == m, blk_f, 1e9), axis=0, keepdims=True)
        hit = blk_f == first
        sel = jnp.where(hit, 1.0, sel)
        score = jnp.where(hit, LOWEST, score)
    selt_ref[...] = sel


def nsa_cmp_attention(q, kvc, overlap, n_c, n_s, tq=ROW_TILE):
    b, t, _ = q.shape
    g = NSA_KV_GROUPS
    ncp = kvc.shape[3]
    assert ncp == LANES, "compressed blocks are laid out on one lane tile"
    assert n_s % SUBLANES == 0 and n_s <= LANES
    gw = NSA_REP * NSA_HEAD_DIM
    kern = functools.partial(_cmp_kernel, n_c=n_c, n_s=n_s, tq=tq)
    return pl.pallas_call(
        kern,
        grid=(b, g, t // tq),
        in_specs=[pl.BlockSpec((None, tq, gw), lambda i, j, k: (i, k, j)),
                  pl.BlockSpec((None, None, None, ncp, NSA_HEAD_DIM), lambda i, j, k: (i, 0, j, 0, 0)),
                  pl.BlockSpec((None, None, None, ncp, NSA_HEAD_DIM), lambda i, j, k: (i, 1, j, 0, 0)),
                  pl.BlockSpec((LANES, LANES), lambda i, j, k: (0, 0))],
        out_specs=[pl.BlockSpec((None, tq, gw), lambda i, j, k: (i, k, j)),
                   pl.BlockSpec((None, None, n_s, tq), lambda i, j, k: (i, j, 0, k))],
        out_shape=[jax.ShapeDtypeStruct((b, t, g * gw), bf16),
                   jax.ShapeDtypeStruct((b, g, n_s, t), f32)],
        compiler_params=_cparams("parallel", "parallel", "parallel"),
        name="nsa_cmp",
    )(q, kvc, kvc, overlap)


def _selwin_kernel(q_ref, ks_ref, vs_ref, kw_ref, vw_ref, selt_ref, oc_ref, gt_ref, o_ref, *, tq, tk):
    g = pl.program_id(1)
    q0 = pl.program_id(2) * tq
    rep = NSA_REP
    dh = NSA_HEAD_DIM
    nq = rep * tq
    span = WINDOW + tq
    c_qk = LOG2E * dh ** -0.5
    q4 = jnp.concatenate([q_ref[:, r * dh:(r + 1) * dh] for r in range(rep)], axis=0)
    nslope = [-LOG2E * _head_slopes(jnp.full((1, 1), r, jnp.int32), g) for r in range(rep)]

    def scores(k, dist, ok):
        bias = jnp.concatenate([jnp.where(ok, nslope[r] * dist, NEG_BIG) for r in range(rep)], axis=1)
        return _dot_nt(k, q4) * c_qk + bias

    def rel_pos(rows):
        return (q0 + lax.broadcasted_iota(jnp.int32, (rows, tq), 1)
                - lax.broadcasted_iota(jnp.int32, (rows, tq), 0))

    d_sel = rel_pos(tk)
    per_tile = tk // SEL_BLOCK

    def sel_body(kt, carry):
        m, l, acc = carry
        k0 = pl.multiple_of(kt * tk, tk)
        dist = (d_sel - k0).astype(f32)
        flags = selt_ref[pl.ds(pl.multiple_of(kt * per_tile, per_tile), per_tile), :]
        picked = jnp.concatenate([jnp.broadcast_to(flags[i:i + 1, :], (SEL_BLOCK, tq))
                                  for i in range(per_tile)], axis=0)
        ok = jnp.where(dist >= 0, picked, 0.0) > 0.5
        s = scores(ks_ref[pl.ds(k0, tk), :], dist, ok)
        m_new = jnp.maximum(m, jnp.max(s, axis=0, keepdims=True))
        alpha = jnp.exp2(m - m_new)
        p = jnp.exp2(s - m_new)
        l = alpha * l + jnp.sum(p, axis=0, keepdims=True)
        acc = alpha * acc + _dot_tn(vs_ref[pl.ds(k0, tk), :], p.astype(bf16))
        return m_new, l, acc

    init = (jnp.full((1, nq), NEG_BIG, f32), jnp.zeros((1, nq), f32), jnp.zeros((dh, nq), f32))
    _, l, acc = lax.fori_loop(0, (q0 + tq + tk - 1) // tk, sel_body, init)
    ot_sel = acc / jnp.maximum(l, TINY)

    k_lo = pl.multiple_of(jnp.maximum(q0 - WINDOW, 0), tq)
    dist = (rel_pos(span) - k_lo).astype(f32)
    ok = jnp.abs(dist - 0.5 * (WINDOW - 1)) <= 0.5 * (WINDOW - 1)
    s = scores(kw_ref[pl.ds(k_lo, span), :], dist, ok)
    m = jnp.max(s, axis=0, keepdims=True)
    p = jnp.exp2(s - m)
    l = jnp.sum(p, axis=0, keepdims=True)
    ot_win = _dot_tn(vw_ref[pl.ds(k_lo, span), :], p.astype(bf16)) / jnp.maximum(l, TINY)

    gate = _sigmoid(gt_ref[...])
    for r in range(rep):
        oc = oc_ref[:, r * dh:(r + 1) * dh].astype(f32)
        os_ = ot_sel[:, r * tq:(r + 1) * tq].T
        ow = ot_win[:, r * tq:(r + 1) * tq].T
        o = (gate[:, r:r + 1] * oc + gate[:, rep + r:rep + r + 1] * os_
             + gate[:, 2 * rep + r:2 * rep + r + 1] * ow)
        o_ref[:, r * dh:(r + 1) * dh] = o.astype(o_ref.dtype)


def nsa_selwin_attention(qkv, sel, o_cmp, gates, tq=ATT_TQ, tk=ATT_TK):
    b, t, _ = qkv.shape
    g = NSA_KV_GROUPS
    gw = NSA_REP * NSA_HEAD_DIM
    n_s = sel.shape[2]
    assert t % tk == 0 and t % tq == 0 and WINDOW % tq == 0 and t >= WINDOW + tq
    assert tk % (SEL_BLOCK * SUBLANES) == 0 and n_s * SEL_BLOCK == t

    def kvspec(branch):
        return pl.BlockSpec((None, t, NSA_HEAD_DIM),
                            lambda i, j, k: (i, 0, NSA_HEADS + branch * g + j))

    kern = functools.partial(_selwin_kernel, tq=tq, tk=tk)
    return pl.pallas_call(
        kern,
        grid=(b, g, t // tq),
        in_specs=[pl.BlockSpec((None, tq, gw), lambda i, j, k: (i, k, j)),
                  kvspec(2), kvspec(3), kvspec(4), kvspec(5),
                  pl.BlockSpec((None, None, n_s, tq), lambda i, j, k: (i, j, 0, k)),
                  pl.BlockSpec((None, tq, gw), lambda i, j, k: (i, k, j)),
                  pl.BlockSpec((None, tq, LANES), lambda i, j, k: (i, k, j))],
        out_specs=pl.BlockSpec((None, tq, gw), lambda i, j, k: (i, k, j)),
        out_shape=jax.ShapeDtypeStruct(o_cmp.shape, bf16),
        compiler_params=_cparams("parallel", "parallel", "parallel"),
        name="nsa_selwin",
    )(qkv, qkv, qkv, qkv, qkv, sel, o_cmp, gates)


def _hgrn_tables(c):
    levels = []
    m = c // 2
    while m >= 1:
        levels.append(m)
        m //= 2
    t = np.arange(c)[:, None]
    u = np.arange(c)[None, :]
    mats = [(u <= t)]
    masks = []
    for li, m in enumerate(levels):
        ref = (t // (2 * m)) * 2 * m + m - 1
        right = (t % (2 * m)) >= m
        if li >= HGRN_BCAST_LEVELS:
            mats.append(u <= ref)
        s = np.arange(c)[None, :]
        masks.append((t // (2 * m) == s // (2 * m)) & right & ((s % (2 * m)) < m))
    lstack = np.concatenate([x.astype(np.float32) for x in mats], axis=0)
    lstack = np.concatenate([lstack, lstack], axis=1)
    return levels, lstack, np.stack([x.astype(np.float32) for x in masks])


def _hgrn_kernel(q_ref, f_ref, i_ref, g_ref, la_ref, lc_ref, oml_ref, ng_ref, ls_ref, mk_ref,
                 o_ref, expo_ref, st_ref, *, c, n_levels):
    d = HGRN_DIM

    @pl.when(pl.program_id(1) == 0)
    def _():
        st_ref[...] = jnp.zeros(st_ref.shape, f32)

    z = f_ref[...]
    log_sig = jnp.minimum(z, 0.0) - jnp.log(1.0 + jnp.exp(-jnp.abs(z)))
    y = lc_ref[...] + log_sig
    a = la_ref[...]
    log_f = jnp.maximum(a, y) + jnp.log(1.0 + jnp.exp(-jnp.abs(a - y)))
    kk = oml_ref[...] * (1.0 / (1.0 + jnp.exp(z)))
    qf = _silu(q_ref[...])
    v = i_ref[...]
    gt = _silu(g_ref[...])
    lf2 = log_f * LOG2E
    hi = lf2.astype(bf16)
    lo = (lf2 - hi.astype(f32)).astype(bf16)
    expo_ref[...] = _dot(ls_ref[...], jnp.concatenate([hi, lo], axis=0))
    ng = ng_ref[...]
    row = lax.broadcasted_iota(jnp.int32, (c, d), 0)
    for h in range(HGRN_HEADS):
        sl = slice(h * d, (h + 1) * d)
        b = expo_ref[0:c, sl]
        qh = qf[:, sl]
        kh = kk[:, sl]
        vh = v[:, sl]
        vb = vh.astype(bf16)
        st = st_ref[h]
        o = _dot_nt((qh * jnp.exp2(b)).astype(bf16), st.astype(bf16))
        amat = jnp.zeros((c, c), f32)
        for li in range(n_levels):
            m = c >> (li + 1)
            if li < HGRN_BCAST_LEVELS:
                bref = b[m - 1:m, :]
                for blk in range(1, 1 << li):
                    bref = jnp.where(row >= 2 * m * blk, b[2 * m * blk + m - 1:2 * m * blk + m, :], bref)
            else:
                k0 = (1 + li - HGRN_BCAST_LEVELS) * c
                bref = expo_ref[k0:k0 + c, sl]
            e = jnp.exp2(-jnp.abs(b - bref))
            part = _dot_nt((qh * e).astype(bf16), (kh * e).astype(bf16))
            amat = amat + mk_ref[li] * part
        diag = jnp.sum(qh * kh, axis=-1, keepdims=True)
        o = o + _dot(amat.astype(bf16), vb) + diag * vh
        b_last = b[c - 1:c, :]
        st_ref[h] = st * jnp.exp2(b_last) + _dot_tn(vb, (kh * jnp.exp2(b_last - b)).astype(bf16))
        o = o * lax.rsqrt(jnp.mean(o * o, axis=-1, keepdims=True) + EPS) * ng
        o_ref[:, sl] = (o * gt[:, sl]).astype(o_ref.dtype)


def hgrn2(proj, col0, log_lb, log1m_lb, one_m_lb, norm_g, c=HGRN_CHUNK):
    b, t, _ = proj.shape
    hd = HGRN_HEADS * HGRN_DIM
    levels, lstack, masks = _hgrn_tables(c)
    cb = col0 // hd
    assert col0 % hd == 0 and t % c == 0
    nl = len(levels)
    kern = functools.partial(_hgrn_kernel, c=c, n_levels=nl)

    def seg(k):
        return pl.BlockSpec((None, c, hd), lambda i, j: (i, j, cb + k))

    def row(width):
        return pl.BlockSpec((1, width), lambda i, j: (0, 0))

    return pl.pallas_call(
        kern,
        grid=(b, t // c),
        in_specs=[seg(0), seg(1), seg(2), seg(3), row(hd), row(hd), row(hd), row(HGRN_DIM),
                  pl.BlockSpec(lstack.shape, lambda i, j: (0, 0)),
                  pl.BlockSpec(masks.shape, lambda i, j: (0, 0, 0))],
        out_specs=pl.BlockSpec((None, c, hd), lambda i, j: (i, j, 0)),
        out_shape=jax.ShapeDtypeStruct((b, t, hd), bf16),
        scratch_shapes=[pltpu.VMEM((lstack.shape[0], hd), f32),
                        pltpu.VMEM((HGRN_HEADS, HGRN_DIM, HGRN_DIM), f32)],
        compiler_params=_cparams("parallel", "arbitrary"),
        name="hgrn2",
    )(proj, proj, proj, proj, log_lb, log1m_lb, one_m_lb, norm_g,
      jnp.asarray(lstack, bf16), jnp.asarray(masks, f32))


def _conv_kernel(x_ref, w_ref, b_ref, g_ref, bb_ref, o_ref, buf_ref, acc_ref, *, tt, sub):
    halo = CONV_HALO
    ch = x_ref.shape[-1]

    @pl.when(pl.program_id(1) == 0)
    def _():
        buf_ref[0:halo, :] = jnp.zeros((halo, ch), f32)

    buf_ref[halo:halo + tt, :] = x_ref[...]
    base = halo - (CONV_WIDTH - 1)

    def lane_block(cb, carry):
        c0 = pl.multiple_of(cb * LANES, LANES)
        w = w_ref[:, pl.ds(c0, LANES)]
        for ts in range(tt // sub):
            acc = jnp.zeros((sub, LANES), f32) + b_ref[:, pl.ds(c0, LANES)]
            for phase in range(SUBLANES):
                taps = [j for j in range(CONV_WIDTH) if (base + j) % SUBLANES == phase]
                reach = max((base + j) // SUBLANES for j in taps) * SUBLANES
                if phase == 0:
                    slab = buf_ref[pl.ds(ts * sub, sub + reach), pl.ds(c0, LANES)]
                else:
                    rows = sub + reach + SUBLANES
                    slab = pltpu.roll(buf_ref[pl.ds(ts * sub, rows), pl.ds(c0, LANES)], rows - phase, 0)
                for j in taps:
                    off = (base + j) // SUBLANES * SUBLANES
                    acc = acc + w[j:j + 1, :] * slab[off:off + sub, :]
            acc_ref[ts * sub:(ts + 1) * sub, pl.ds(c0, LANES)] = acc
        return carry

    lax.fori_loop(0, ch // LANES, lane_block, 0)
    buf_ref[0:halo, :] = buf_ref[tt:tt + halo, :]
    y = acc_ref[...]
    mu = jnp.mean(y, axis=-1, keepdims=True)
    yc = y - mu
    var = jnp.mean(yc * yc, axis=-1, keepdims=True)
    yn = yc * lax.rsqrt(var + EPS) * g_ref[...] + bb_ref[...]
    o_ref[...] = _silu(yn).astype(o_ref.dtype)


def conv_ln_swish(x, w_dw, b_dw, ln_g, ln_b, tt=CONV_TT, sub=64):
    b, t, ch = x.shape
    tt = min(tt, t)
    assert t % tt == 0 and tt % sub == 0 and tt >= CONV_HALO
    wpad = jnp.zeros((CONV_HALO, ch), f32).at[:CONV_WIDTH].set(w_dw)
    kern = functools.partial(_conv_kernel, tt=tt, sub=sub)

    def row():
        return pl.BlockSpec((1, ch), lambda i, j: (0, 0))

    return pl.pallas_call(
        kern,
        grid=(b, t // tt),
        in_specs=[pl.BlockSpec((None, tt, ch), lambda i, j: (i, j, 0)),
                  pl.BlockSpec((CONV_HALO, ch), lambda i, j: (0, 0)), row(), row(), row()],
        out_specs=pl.BlockSpec((None, tt, ch), lambda i, j: (i, j, 0)),
        out_shape=jax.ShapeDtypeStruct((b, t, ch), bf16),
        scratch_shapes=[pltpu.VMEM((tt + CONV_HALO, ch), f32), pltpu.VMEM((tt, ch), f32)],
        compiler_params=_cparams("parallel", "arbitrary"),
        name="conv_ln_swish",
    )(x, wpad, b_dw.reshape(1, ch), ln_g.reshape(1, ch), ln_b.reshape(1, ch))


def _router_kernel(x_ref, g_ref, w_ref, tri_ref, u_ref, meta_ref, cnt_ref, run_ref):
    @pl.when(pl.program_id(0) == 0)
    def _():
        run_ref[...] = jnp.zeros(run_ref.shape, f32)

    x = x_ref[...]
    u = x * lax.rsqrt(jnp.mean(x * x, axis=-1, keepdims=True) + EPS) * g_ref[...]
    u_ref[...] = u.astype(u_ref.dtype)
    uh, um, _ = _split3(u)
    wh, wm, _ = _split3(w_ref[...])
    logits = _dot(uh, wh) + (_dot(uh, wm) + _dot(um, wh))
    lane = lax.broadcasted_iota(jnp.int32, (1, LANES), 1)
    lane_f = lane.astype(f32)
    l1 = jnp.where(lane < N_EXPERTS, logits, LOWEST)
    m1 = jnp.max(l1, axis=-1, keepdims=True)
    i1 = jnp.min(jnp.where(l1 == m1, lane_f, 1e9), axis=-1, keepdims=True)
    l2 = jnp.where(lane_f == i1, LOWEST, l1)
    m2 = jnp.max(l2, axis=-1, keepdims=True)
    i2 = jnp.min(jnp.where(l2 == m2, lane_f, 1e9), axis=-1, keepdims=True)
    e2 = jnp.exp(m2 - m1)
    g1 = 1.0 / (1.0 + e2)
    g2 = e2 / (1.0 + e2)
    hit1 = lane_f == i1
    hit2 = lane_f == i2
    onehot = jnp.where(hit1 | hit2, 1.0, 0.0)
    cum = _dot(tri_ref[...], onehot.astype(bf16)) + run_ref[...]
    r1 = jnp.sum(jnp.where(hit1, cum, 0.0), axis=-1, keepdims=True)
    r2 = jnp.sum(jnp.where(hit2, cum, 0.0), axis=-1, keepdims=True)
    run = run_ref[...] + jnp.sum(onehot, axis=0, keepdims=True)
    run_ref[...] = run
    cnt_ref[...] = run
    meta = jnp.zeros(meta_ref.shape, f32)
    for col, val in enumerate((i1, i2, g1, g2, r1, r2)):
        meta = jnp.where(lane == col, val, meta)
    meta_ref[...] = meta


def moe_router(h, norm_g, w_router, tm=ROUTER_TM):
    m, d = h.shape
    tm = min(tm, m)
    wpad = jnp.zeros((d, LANES), f32).at[:, :N_EXPERTS].set(w_router)
    tri = jnp.asarray(np.tril(np.ones((tm, tm), np.float32), -1), bf16)
    return pl.pallas_call(
        _router_kernel,
        grid=(m // tm,),
        in_specs=[pl.BlockSpec((tm, d), lambda i: (i, 0)),
                  pl.BlockSpec((1, d), lambda i: (0, 0)),
                  pl.BlockSpec((d, LANES), lambda i: (0, 0)),
                  pl.BlockSpec((tm, tm), lambda i: (0, 0))],
        out_specs=[pl.BlockSpec((tm, d), lambda i: (i, 0)),
                   pl.BlockSpec((tm, LANES), lambda i: (i, 0)),
                   pl.BlockSpec((1, LANES), lambda i: (0, 0))],
        out_shape=[jax.ShapeDtypeStruct((m, d), bf16),
                   jax.ShapeDtypeStruct((m, LANES), f32),
                   jax.ShapeDtypeStruct((1, LANES), f32)],
        scratch_shapes=[pltpu.VMEM((1, LANES), f32)],
        compiler_params=_cparams("arbitrary"),
        name="moe_router",
    )(h, norm_g.reshape(1, d), wpad, tri)


def _dispatch_kernel(dest_ref, u_ref, xs_ref, zero_ref, sem, zsem, *, tm):
    @pl.when(pl.program_id(0) == 0)
    def _():
        zero_ref[...] = jnp.zeros(zero_ref.shape, zero_ref.dtype)
        zr = zero_ref.shape[0]
        nz = xs_ref.shape[0] // zr

        def fill(c, carry):
            pltpu.make_async_copy(zero_ref, xs_ref.at[pl.ds(c * zr, zr)], zsem).start()
            return carry

        lax.fori_loop(0, nz, fill, 0)

        def fill_done(c, carry):
            pltpu.make_async_copy(zero_ref, xs_ref.at[pl.ds(0, zr)], zsem).wait()
            return carry

        lax.fori_loop(0, nz, fill_done, 0)

    def issue(r, carry):
        for k in range(2):
            pltpu.make_async_copy(u_ref.at[r], xs_ref.at[dest_ref[0, 2 * r + k]], sem).start()
        return carry

    lax.fori_loop(0, tm, issue, 0)

    def drain(r, carry):
        for k in range(2):
            pltpu.make_async_copy(u_ref.at[0], xs_ref.at[0], sem).wait()
        return carry

    lax.fori_loop(0, tm, drain, 0)


def moe_dispatch(u3, dest, p_rows, tm=ROW_TILE):
    n, s, _ = u3.shape
    tm = min(tm, n)
    zr = _largest_tile(p_rows, 512, 1)
    kern = functools.partial(_dispatch_kernel, tm=tm)
    return pl.pallas_call(
        kern,
        grid=(n // tm,),
        in_specs=[pl.BlockSpec((None, 1, 2 * tm), lambda i: (i, 0, 0), memory_space=pltpu.SMEM),
                  pl.BlockSpec((tm, s, LANES), lambda i: (i, 0, 0))],
        out_specs=pl.BlockSpec(memory_space=pl.ANY),
        out_shape=jax.ShapeDtypeStruct((p_rows, s, LANES), u3.dtype),
        scratch_shapes=[pltpu.VMEM((zr, s, LANES), u3.dtype), pltpu.SemaphoreType.DMA(()),
                        pltpu.SemaphoreType.DMA(())],
        compiler_params=_cparams("arbitrary"),
        name="moe_dispatch",
    )(dest.reshape(n // tm, 1, 2 * tm), u3)


def _combine_kernel(dest_ref, gate_ref, y_ref, h_ref, o_ref, buf_ref, sem, *, tm):
    def issue(r, carry):
        for k in range(2):
            pltpu.make_async_copy(y_ref.at[dest_ref[0, 2 * r + k]], buf_ref.at[k, r], sem).start()
        return carry

    lax.fori_loop(0, tm, issue, 0)

    def drain(r, carry):
        for k in range(2):
            pltpu.make_async_copy(y_ref.at[0], buf_ref.at[0, 0], sem).wait()
        return carry

    lax.fori_loop(0, tm, drain, 0)

    def mix(r, carry):
        o_ref[r] = (h_ref[r] + gate_ref[0, 2 * r] * buf_ref[0, r]
                    + gate_ref[0, 2 * r + 1] * buf_ref[1, r])
        return carry

    lax.fori_loop(0, tm, mix, 0)


def moe_combine(y3, dest, gate, h3, tm=ROW_TILE):
    n, s, _ = h3.shape
    tm = min(tm, n)
    kern = functools.partial(_combine_kernel, tm=tm)
    return pl.pallas_call(
        kern,
        grid=(n // tm,),
        in_specs=[pl.BlockSpec((None, 1, 2 * tm), lambda i: (i, 0, 0), memory_space=pltpu.SMEM),
                  pl.BlockSpec((None, 1, 2 * tm), lambda i: (i, 0, 0), memory_space=pltpu.SMEM),
                  pl.BlockSpec(memory_space=pl.ANY),
                  pl.BlockSpec((tm, s, LANES), lambda i: (i, 0, 0))],
        out_specs=pl.BlockSpec((tm, s, LANES), lambda i: (i, 0, 0)),
        out_shape=jax.ShapeDtypeStruct(h3.shape, f32),
        scratch_shapes=[pltpu.VMEM((2, tm, s, LANES), f32), pltpu.SemaphoreType.DMA(())],
        compiler_params=_cparams("arbitrary"),
        name="moe_combine",
    )(dest.reshape(n // tm, 1, 2 * tm), gate.reshape(n // tm, 1, 2 * tm), y3, h3)


def _mixer_ab(h2, bsz, t, norm_g, w_in, w_out, pos_k, pos_v, w1_k, w1_v, w2_k, w2_v, lb, hgrn_norm):
    m, d = h2.shape
    nq = NSA_HEADS * NSA_HEAD_DIM
    nkv = NSA_KV_GROUPS * NSA_HEAD_DIM
    hd = HGRN_HEADS * HGRN_DIM
    g, rep = NSA_KV_GROUPS, NSA_REP
    u = rmsnorm(h2, norm_g, bf16)
    o_kv = nq
    o_gate = o_kv + 6 * nkv
    o_hg = o_gate + 3 * NSA_HEADS
    w_attn = w_in[:, :o_gate].astype(bf16)[None]
    w_hg = w_in[:, o_hg:o_hg + 4 * hd].astype(bf16)[None]
    wg = w_in[:, o_gate:o_hg].reshape(d, g, rep, 3).transpose(0, 1, 3, 2).reshape(d, g, 3 * rep)
    wg = jnp.pad(wg, ((0, 0), (0, 0), (0, LANES - 3 * rep))).reshape(d, g * LANES).astype(bf16)[None]

    attn = matmul(u, w_attn, out_dtype=bf16)
    hg = matmul(u, w_hg, out_dtype=f32, tn=1024)
    gates = matmul(u, wg, out_dtype=f32, tn=g * LANES)

    qkv = attn.reshape(bsz, t, nq + 6 * nkv)
    n_c = (t - CMP_BLOCK) // CMP_STRIDE + 1
    n_s = t // SEL_BLOCK
    rows = t // CMP_STRIDE
    xg = qkv[:, :, nq:nq + 2 * nkv].reshape(bsz, t, 2, g, NSA_HEAD_DIM).transpose(0, 2, 3, 1, 4)
    xg = xg.reshape(bsz, 2, g, rows, CMP_STRIDE * NSA_HEAD_DIM)
    pos = jnp.stack([pos_k, pos_v]).reshape(2, 1, CMP_BLOCK * NSA_HEAD_DIM)
    kvc = nsa_compress(xg, pos, jnp.stack([w1_k, w1_v]).astype(bf16),
                       jnp.stack([w2_k, w2_v]).astype(bf16), n_c)
    cj = np.arange(LANES)[:, None] * CMP_STRIDE
    sk = np.arange(LANES)[None, :] * SEL_BLOCK
    overlap = ((cj < sk + SEL_BLOCK) & (cj + CMP_BLOCK > sk)
               & (np.arange(LANES)[:, None] < n_c) & (np.arange(LANES)[None, :] < n_s))
    o_cmp, sel = nsa_cmp_attention(qkv, kvc, jnp.asarray(overlap.astype(np.float32), bf16), n_c, n_s)
    o_a = nsa_selwin_attention(qkv, sel, o_cmp, gates.reshape(bsz, t, g * LANES))

    lbc = jnp.clip(lb.astype(f32), 0.0, LB_MAX).reshape(1, hd)
    o_b = hgrn2(hg.reshape(bsz, t, 4 * hd), 0, jnp.log(jnp.maximum(lbc, TINY)), jnp.log1p(-lbc),
                1.0 - lbc, hgrn_norm.reshape(1, HGRN_DIM))
    mix = jnp.concatenate([o_a.reshape(m, nq), o_b.reshape(m, hd)], axis=-1)
    return matmul(mix, w_out.astype(bf16)[None], residual=h2, out_dtype=f32, tn=1024)


def _dense_ffn(h2, norm_g, w1_all, w3_all, w2, j, tm=1024):
    u = rmsnorm(h2, norm_g, bf16)
    tm = _largest_tile(h2.shape[0], tm, SUBLANES)
    nblk = h2.shape[0] // tm
    mid = matmul_gated(u, w1_all, w3_all, act_a="silu", tm=tm, n_outer=True,
                       blk_e=jnp.full((nblk,), j, jnp.int32), n_used=jnp.full((1,), nblk, jnp.int32))
    return matmul(mid, w2.astype(bf16)[None], residual=h2, out_dtype=f32)


def _conformer(h2, bsz, t, norm_g, w_pw1, b_pw1, w_dw, b_dw, ln_g, ln_b, w_pw2, b_pw2):
    m, d = h2.shape
    ch = w_dw.shape[-1]
    u = rmsnorm(h2, norm_g, bf16)
    glu = matmul_gated(u, w_pw1[:, :ch].astype(bf16)[None], w_pw1[:, ch:].astype(bf16)[None],
                       ba=b_pw1[:ch].reshape(1, ch), bb=b_pw1[ch:].reshape(1, ch),
                       act_b="sigmoid", out_dtype=f32, tn=1024)
    y = conv_ln_swish(glu.reshape(bsz, t, ch), w_dw, b_dw, ln_g, ln_b)
    return matmul(y.reshape(m, ch), w_pw2.astype(bf16)[None], bias=b_pw2.reshape(1, d),
                  residual=h2, out_dtype=f32, tn=1024)


def _moe(h2, norm_g, w_router, w1_all, w3_all, w2_all, j, tm=MOE_TM):
    n, d = h2.shape
    e = N_EXPERTS
    s = d // LANES
    u, meta, counts = moe_router(h2, norm_g, w_router)
    tm = min(tm, n)
    idx = meta[:, 0:2].astype(jnp.int32)
    gate = meta[:, 2:4]
    rank = meta[:, 4:6].astype(jnp.int32)
    cnt = counts[0, :e].astype(jnp.int32)
    nblk = (cnt + tm - 1) // tm
    blk_end = jnp.cumsum(nblk)
    pstart = (blk_end - nblk) * tm
    dest = pstart[idx] + rank
    total_blk = (2 * n) // tm + e
    p_rows = total_blk * tm
    blk_e = jnp.minimum(jnp.searchsorted(blk_end, jnp.arange(total_blk), side="right"), e - 1)
    blk_e = blk_e.astype(jnp.int32)
    n_used = blk_end[-1:].astype(jnp.int32)

    xs = moe_dispatch(u.reshape(n, s, LANES), dest, p_rows).reshape(p_rows, d)
    mid = matmul_gated(xs, w1_all, w3_all, act_a="silu", tm=tm, tn=1024, n_outer=True,
                       blk_e=blk_e + j * e, n_used=n_used)
    y = matmul(mid, w2_all, out_dtype=f32, tm=tm, blk_e=blk_e + j * e, n_used=n_used, n_outer=True)
    out = moe_combine(y.reshape(p_rows, s, LANES), dest, gate, h2.reshape(n, s, LANES))
    return out.reshape(n, d)


def _lower_bounds(table):
    p = jax.nn.softmax(table.astype(f32), axis=0)
    return jnp.cumsum(p, axis=0) - p[0]


def kernel(x, norm_mix, norm_ffn, final_norm, ab_w_in, ab_w_out, cmp_pos_k, cmp_pos_v, cmp_w1_k, cmp_w1_v, cmp_w2_k, cmp_w2_v, hgrn_lower_bounds, hgrn_norm, ffn_w1, ffn_w3, ffn_w2, conv_w_pw1, conv_b_pw1, conv_w_dw, conv_b_dw, conv_ln_g, conv_ln_b, conv_w_pw2, conv_b_pw2, moe_router, moe_w1, moe_w3, moe_w2):
    bsz, t, d = x.shape
    depth = norm_mix.shape[0]
    lbs = _lower_bounds(hgrn_lower_bounds)
    h = x.reshape(bsz * t, d)
    for layer in range(depth):
        j = layer // 2
        if layer % 2 == 0:
            h = _mixer_ab(h, bsz, t, norm_mix[layer], ab_w_in[j], ab_w_out[j], cmp_pos_k[j],
                          cmp_pos_v[j], cmp_w1_k[j], cmp_w1_v[j], cmp_w2_k[j], cmp_w2_v[j],
                          lbs[j], hgrn_norm[j])
            h = _dense_ffn(h, norm_ffn[layer], ffn_w1, ffn_w3, ffn_w2[j], j)
        else:
            h = _conformer(h, bsz, t, norm_mix[layer], conv_w_pw1[j], conv_b_pw1[j], conv_w_dw[j],
                           conv_b_dw[j], conv_ln_g[j], conv_ln_b[j], conv_w_pw2[j], conv_b_pw2[j])
            fe = moe_w1.shape[-1]
            h = _moe(h, norm_ffn[layer], moe_router[j], moe_w1.reshape(-1, d, fe),
                     moe_w3.reshape(-1, d, fe), moe_w2.reshape(-1, fe, d), j)
    return rmsnorm(h, final_norm, f32).reshape(bsz, t, d)
```

```python
import functools

import numpy as np
import jax
import jax.numpy as jnp
from jax import lax
from jax.experimental import pallas as pl
from jax.experimental.pallas import tpu as pltpu

f32 = jnp.float32
bf16 = jnp.bfloat16

NSA_HEADS = 8
NSA_KV_GROUPS = 2
NSA_HEAD_DIM = 128
NSA_REP = NSA_HEADS // NSA_KV_GROUPS
CMP_BLOCK = 32
CMP_STRIDE = 16
CMP_HIDDEN = 256
SEL_BLOCK = 64
SEL_TOPN = 8
WINDOW = 512
FORCE_BONUS = 1.0e4
HGRN_HEADS = 8
HGRN_DIM = 128
LB_MAX = 1.0 - 1e-6
CONV_WIDTH = 31
N_EXPERTS = 8
EPS = 1e-6
NEG_BIG = -1e30
TINY = 1e-30
LOWEST = -3.0e38
LOG2E = 1.4426950408889634

LANES = 128
SUBLANES = 8
VMEM_LIMIT = 56 * 1024 * 1024

HGRN_CHUNK = 128
HGRN_BCAST_LEVELS = 2
ATT_TQ = 128
ATT_TK = 512
CONV_TT = 256
CONV_HALO = 32
MOE_TM = 512
ROUTER_TM = 512
ROW_TILE = 256


def _cparams(*sem):
    return pltpu.CompilerParams(dimension_semantics=sem, vmem_limit_bytes=VMEM_LIMIT)


def _largest_tile(n, cap, unit):
    if n <= cap:
        return n
    best = max(k for k in range(unit, cap + 1, unit) if n % k == 0)
    return best


def _dot(a, b):
    return jnp.dot(a, b, preferred_element_type=f32)


def _dot_nt(a, b):
    return lax.dot_general(a, b, (((1,), (1,)), ((), ())), preferred_element_type=f32)


def _dot_tn(a, b):
    return lax.dot_general(a, b, (((0,), (0,)), ((), ())), preferred_element_type=f32)


def _sigmoid(x):
    return 1.0 / (1.0 + jnp.exp(-x))


def _silu(x):
    return x * _sigmoid(x)


def _split3(x):
    hi = x.astype(bf16)
    r = x - hi.astype(f32)
    mid = r.astype(bf16)
    lo = (r - mid.astype(f32)).astype(bf16)
    return hi, mid, lo


def _rmsnorm_kernel(x_ref, g_ref, o_ref):
    x = x_ref[...]
    y = x * lax.rsqrt(jnp.mean(x * x, axis=-1, keepdims=True) + EPS) * g_ref[...]
    o_ref[...] = y.astype(o_ref.dtype)


def rmsnorm(x, g, out_dtype, tm=1024):
    m, d = x.shape
    tm = _largest_tile(m, tm, SUBLANES)
    return pl.pallas_call(
        _rmsnorm_kernel,
        grid=(m // tm,),
        in_specs=[pl.BlockSpec((tm, d), lambda i: (i, 0)),
                  pl.BlockSpec((1, d), lambda i: (0, 0))],
        out_specs=pl.BlockSpec((tm, d), lambda i: (i, 0)),
        out_shape=jax.ShapeDtypeStruct((m, d), out_dtype),
        compiler_params=_cparams("parallel"),
        name="rmsnorm",
    )(x, g.reshape(1, d))


def _mm_kernel(be_ref, nu_ref, x_ref, w_ref, *rest, has_bias, has_res, n_outer, cast_w):
    rest = list(rest)
    b_ref = rest.pop(0) if has_bias else None
    r_ref = rest.pop(0) if has_res else None
    o_ref = rest.pop(0)
    i = pl.program_id(1 if n_outer else 0)
    if cast_w:
        w16_ref, = rest
        fresh = (i == 0) | (be_ref[i] != be_ref[jnp.maximum(i - 1, 0)])

        @pl.when((i < nu_ref[0]) & fresh)
        def _():
            w16_ref[...] = w_ref[...].astype(bf16)

        w_ref = w16_ref

    @pl.when(i < nu_ref[0])
    def _():
        acc = _dot(x_ref[...], w_ref[...])
        if has_bias:
            acc = acc + b_ref[...]
        if has_res:
            acc = acc + r_ref[...]
        o_ref[...] = acc.astype(o_ref.dtype)

    @pl.when(i >= nu_ref[0])
    def _():
        o_ref[...] = jnp.zeros(o_ref.shape, o_ref.dtype)


def _mm_setup(m, n, tm, tn, blk_e, n_used, n_outer):
    tm = _largest_tile(m, tm, SUBLANES)
    tn = _largest_tile(n, tn, LANES)
    nblk = m // tm
    if blk_e is None:
        blk_e = jnp.zeros((nblk,), jnp.int32)
        n_used = jnp.full((1,), nblk, jnp.int32)
    if n_outer:
        grid = (n // tn, nblk)

        def ij(f):
            return lambda j, i, be, nu: f(i, j, be, nu)
    else:
        grid = (nblk, n // tn)

        def ij(f):
            return f
    return tm, tn, grid, ij, blk_e, n_used


def _wmap(i, j, be, nu):
    return (be[i], 0, jnp.where(i < nu[0], j, 0))


def matmul(x, w, *, bias=None, residual=None, out_dtype=f32, tm=1024, tn=512,
           blk_e=None, n_used=None, n_outer=False):
    m, k = x.shape
    n = w.shape[-1]
    cast_w = w.dtype == f32
    assert n_outer or not cast_w
    tm, tn, grid, ij, blk_e, n_used = _mm_setup(m, n, tm, tn, blk_e, n_used, n_outer)
    w_mode = dict(pipeline_mode=pl.Buffered(1)) if cast_w else {}
    in_specs = [pl.BlockSpec((tm, k), ij(lambda i, j, be, nu: (i, 0))),
                pl.BlockSpec((None, k, tn), ij(_wmap), **w_mode)]
    args = [x, w]
    if bias is not None:
        in_specs.append(pl.BlockSpec((1, tn), ij(lambda i, j, be, nu: (0, j))))
        args.append(bias)
    if residual is not None:
        in_specs.append(pl.BlockSpec((tm, tn), ij(lambda i, j, be, nu: (i, j))))
        args.append(residual)
    kern = functools.partial(_mm_kernel, has_bias=bias is not None, has_res=residual is not None,
                             n_outer=n_outer, cast_w=cast_w)
    return pl.pallas_call(
        kern,
        grid_spec=pltpu.PrefetchScalarGridSpec(
            num_scalar_prefetch=2, grid=grid, in_specs=in_specs,
            out_specs=pl.BlockSpec((tm, tn), ij(lambda i, j, be, nu: (i, j))),
            scratch_shapes=[pltpu.VMEM((k, tn), bf16)] if cast_w else []),
        out_shape=jax.ShapeDtypeStruct((m, n), out_dtype),
        compiler_params=_cparams("parallel", "arbitrary"),
        name="matmul",
    )(blk_e, n_used, *args)


def _mm2_kernel(be_ref, nu_ref, x_ref, wa_ref, wb_ref, *rest, has_bias, act_a, act_b, n_outer, cast_w):
    rest = list(rest)
    ba_ref = rest.pop(0) if has_bias else None
    bb_ref = rest.pop(0) if has_bias else None
    o_ref = rest.pop(0)
    i = pl.program_id(1 if n_outer else 0)
    active = i < nu_ref[0]
    if cast_w:
        wa16_ref, wb16_ref = rest
        fresh = (i == 0) | (be_ref[i] != be_ref[jnp.maximum(i - 1, 0)])

        @pl.when(active & fresh)
        def _():
            wa16_ref[...] = wa_ref[...].astype(bf16)
            wb16_ref[...] = wb_ref[...].astype(bf16)

        wa_ref, wb_ref = wa16_ref, wb16_ref

    @pl.when(active)
    def _():
        x = x_ref[...]
        a = _dot(x, wa_ref[...])
        b = _dot(x, wb_ref[...])
        if has_bias:
            a = a + ba_ref[...]
            b = b + bb_ref[...]
        if act_a == "silu":
            a = _silu(a)
        if act_b == "sigmoid":
            b = _sigmoid(b)
        o_ref[...] = (a * b).astype(o_ref.dtype)

    @pl.when(jnp.logical_not(active))
    def _():
        o_ref[...] = jnp.zeros(o_ref.shape, o_ref.dtype)


def matmul_gated(x, wa, wb, *, ba=None, bb=None, act_a=None, act_b=None, out_dtype=bf16,
                 tm=1024, tn=512, blk_e=None, n_used=None, n_outer=False):
    m, k = x.shape
    n = wa.shape[-1]
    cast_w = wa.dtype == f32
    assert n_outer or not cast_w
    tm, tn, grid, ij, blk_e, n_used = _mm_setup(m, n, tm, tn, blk_e, n_used, n_outer)
    in_specs = [pl.BlockSpec((tm, k), ij(lambda i, j, be, nu: (i, 0))),
                pl.BlockSpec((None, k, tn), ij(_wmap)),
                pl.BlockSpec((None, k, tn), ij(_wmap))]
    args = [x, wa, wb]
    if ba is not None:
        in_specs += [pl.BlockSpec((1, tn), ij(lambda i, j, be, nu: (0, j)))] * 2
        args += [ba, bb]
    kern = functools.partial(_mm2_kernel, has_bias=ba is not None, act_a=act_a, act_b=act_b,
                             n_outer=n_outer, cast_w=cast_w)
    return pl.pallas_call(
        kern,
        grid_spec=pltpu.PrefetchScalarGridSpec(
            num_scalar_prefetch=2, grid=grid, in_specs=in_specs,
            out_specs=pl.BlockSpec((tm, tn), ij(lambda i, j, be, nu: (i, j))),
            scratch_shapes=[pltpu.VMEM((k, tn), bf16)] * 2 if cast_w else []),
        out_shape=jax.ShapeDtypeStruct((m, n), out_dtype),
        compiler_params=_cparams("parallel", "arbitrary"),
        name="matmul_gated",
    )(blk_e, n_used, *args)


def _gelu_tanh(x):
    c = np.sqrt(2.0 / np.pi).astype(np.float32)
    return 0.5 * x * (1.0 + jnp.tanh(c * (x + 0.044715 * (x * x * x))))


def _compress_kernel(x_ref, pos_ref, w1_ref, w2_ref, o_ref, *, n_c):
    half = CMP_STRIDE * NSA_HEAD_DIM
    x = x_ref[...].astype(f32)
    rows = x.shape[0]
    xn = pltpu.roll(x, rows - 1, 0)
    pos = pos_ref[...]
    xa = (x + pos[:, :half]).astype(bf16)
    xb = (xn + pos[:, half:]).astype(bf16)
    h = _dot(xa, w1_ref[:half, :]) + _dot(xb, w1_ref[half:, :])
    o = _dot(_gelu_tanh(h).astype(bf16), w2_ref[...])
    ridx = lax.broadcasted_iota(jnp.int32, o.shape, 0)
    o_ref[...] = jnp.where(ridx < n_c, o, 0.0).astype(o_ref.dtype)


def nsa_compress(xg, pos, w1, w2, n_c):
    b, _, g, rows, width = xg.shape
    kern = functools.partial(_compress_kernel, n_c=n_c)
    return pl.pallas_call(
        kern,
        grid=(b, 2, g),
        in_specs=[pl.BlockSpec((None, None, None, rows, width), lambda i, s, j: (i, s, j, 0, 0)),
                  pl.BlockSpec((None, 1, 2 * width), lambda i, s, j: (s, 0, 0)),
                  pl.BlockSpec((None, 2 * width, CMP_HIDDEN), lambda i, s, j: (s, 0, 0)),
                  pl.BlockSpec((None, CMP_HIDDEN, NSA_HEAD_DIM), lambda i, s, j: (s, 0, 0))],
        out_specs=pl.BlockSpec((None, None, None, rows, NSA_HEAD_DIM),
                               lambda i, s, j: (i, s, j, 0, 0)),
        out_shape=jax.ShapeDtypeStruct((b, 2, g, rows, NSA_HEAD_DIM), bf16),
        compiler_params=_cparams("parallel", "parallel", "parallel"),
        name="nsa_compress",
    )(xg, pos, w1, w2)


def _head_slopes(rows_head, g):
    h = g * NSA_REP + rows_head
    out = jnp.zeros(h.shape, f32)
    for hh in range(NSA_HEADS):
        out = jnp.where(h == hh, np.float32(2.0 ** (-8.0 * (hh + 1) / NSA_HEADS)), out)
    return out


def _cmp_kernel(q_ref, kc_ref, vc_ref, ov_ref, o_ref, selt_ref, *, n_c, n_s, tq):
    g = pl.program_id(1)
    q0 = pl.program_id(2) * tq
    scale = np.float32(NSA_HEAD_DIM ** -0.5)
    tpos = q0 + lax.broadcasted_iota(jnp.int32, (tq, 1), 0)
    lane = lax.broadcasted_iota(jnp.int32, (1, LANES), 1)
    c_dist = (tpos - (lane * CMP_STRIDE + (CMP_BLOCK - 1))).astype(f32)
    cmask = (c_dist >= 0) & (lane < n_c)
    kc = kc_ref[...]
    vc = vc_ref[...]
    psum = jnp.zeros((tq, LANES), f32)
    for r in range(NSA_REP):
        slope = _head_slopes(jnp.full((1, 1), r, jnp.int32), g)
        qr = q_ref[:, r * NSA_HEAD_DIM:(r + 1) * NSA_HEAD_DIM]
        s = _dot_nt(qr, kc) * scale - slope * c_dist
        s = jnp.where(cmask, s, NEG_BIG)
        m = jnp.max(s, axis=-1, keepdims=True)
        e = jnp.where(cmask, jnp.exp(s - m), 0.0)
        p = e / jnp.maximum(jnp.sum(e, axis=-1, keepdims=True), TINY)
        o_ref[:, r * NSA_HEAD_DIM:(r + 1) * NSA_HEAD_DIM] = _dot(p.astype(bf16), vc).astype(o_ref.dtype)
        psum = psum + p
    hi, mid, lo = _split3(psum)
    ov = ov_ref[...]
    imp = _dot(hi, ov) + _dot(mid, ov) + _dot(lo, ov)
    cur = tpos // SEL_BLOCK
    valid = lane * SEL_BLOCK <= tpos
    forced = valid & ((lane == 0) | (lane == cur) | (lane == cur - 1))
    score = jnp.where(valid, imp, -1.0) + jnp.where(forced, FORCE_BONUS, 0.0)
    score = score.T[:n_s, :]
    blk_f = lax.broadcasted_iota(jnp.int32, (n_s, tq), 0).astype(f32)
    sel = jnp.zeros((n_s, tq), f32)
    for _ in range(min(SEL_TOPN, n_s)):
        m = jnp.max(score, axis=0, keepdims=True)
        first = jnp.min(jnp.where(score == m, blk_f, 1e9), axis=0, keepdims=True)
        hit = blk_f == first
        sel = jnp.where(hit, 1.0, sel)
        score = jnp.where(hit, LOWEST, score)
    selt_ref[...] = sel


def nsa_cmp_attention(q, kvc, overlap, n_c, n_s, tq=ROW_TILE):
    b, t, _ = q.shape
    g = NSA_KV_GROUPS
    ncp = kvc.shape[3]
    assert ncp == LANES, "compressed blocks are laid out on one lane tile"
    assert n_s % SUBLANES == 0 and n_s <= LANES
    gw = NSA_REP * NSA_HEAD_DIM
    kern = functools.partial(_cmp_kernel, n_c=n_c, n_s=n_s, tq=tq)
    return pl.pallas_call(
        kern,
        grid=(b, g, t // tq),
        in_specs=[pl.BlockSpec((None, tq, gw), lambda i, j, k: (i, k, j)),
                  pl.BlockSpec((None, None, None, ncp, NSA_HEAD_DIM), lambda i, j, k: (i, 0, j, 0, 0)),
                  pl.BlockSpec((None, None, None, ncp, NSA_HEAD_DIM), lambda i, j, k: (i, 1, j, 0, 0)),
                  pl.BlockSpec((LANES, LANES), lambda i, j, k: (0, 0))],
        out_specs=[pl.BlockSpec((None, tq, gw), lambda i, j, k: (i, k, j)),
                   pl.BlockSpec((None, None, n_s, tq), lambda i, j, k: (i, j, 0, k))],
        out_shape=[jax.ShapeDtypeStruct((b, t, g * gw), bf16),
                   jax.ShapeDtypeStruct((b, g, n_s, t), f32)],
        compiler_params=_cparams("parallel", "parallel", "parallel"),
        name="nsa_cmp",
    )(q, kvc, kvc, overlap)


def _selwin_kernel(q_ref, ks_ref, vs_ref, kw_ref, vw_ref, selt_ref, oc_ref, gt_ref, o_ref, *, tq, tk):
    g = pl.program_id(1)
    q0 = pl.program_id(2) * tq
    rep = NSA_REP
    dh = NSA_HEAD_DIM
    nq = rep * tq
    span = WINDOW + tq
    c_qk = LOG2E * dh ** -0.5
    q4 = jnp.concatenate([q_ref[:, r * dh:(r + 1) * dh] for r in range(rep)], axis=0)
    nslope = [-LOG2E * _head_slopes(jnp.full((1, 1), r, jnp.int32), g) for r in range(rep)]

    def scores(k, dist, ok):
        bias = jnp.concatenate([jnp.where(ok, nslope[r] * dist, NEG_BIG) for r in range(rep)], axis=1)
        return _dot_nt(k, q4) * c_qk + bias

    def rel_pos(rows):
        return (q0 + lax.broadcasted_iota(jnp.int32, (rows, tq), 1)
                - lax.broadcasted_iota(jnp.int32, (rows, tq), 0))

    d_sel = rel_pos(tk)
    per_tile = tk // SEL_BLOCK

    def sel_body(kt, carry):
        m, l, acc = carry
        k0 = pl.multiple_of(kt * tk, tk)
        dist = (d_sel - k0).astype(f32)
        flags = selt_ref[pl.ds(pl.multiple_of(kt * per_tile, per_tile), per_tile), :]
        picked = jnp.concatenate([jnp.broadcast_to(flags[i:i + 1, :], (SEL_BLOCK, tq))
                                  for i in range(per_tile)], axis=0)
        ok = jnp.where(dist >= 0, picked, 0.0) > 0.5
        s = scores(ks_ref[pl.ds(k0, tk), :], dist, ok)
        m_new = jnp.maximum(m, jnp.max(s, axis=0, keepdims=True))
        alpha = jnp.exp2(m - m_new)
        p = jnp.exp2(s - m_new)
        l = alpha * l + jnp.sum(p, axis=0, keepdims=True)
        acc = alpha * acc + _dot_tn(vs_ref[pl.ds(k0, tk), :], p.astype(bf16))
        return m_new, l, acc

    init = (jnp.full((1, nq), NEG_BIG, f32), jnp.zeros((1, nq), f32), jnp.zeros((dh, nq), f32))
    _, l, acc = lax.fori_loop(0, (q0 + tq + tk - 1) // tk, sel_body, init)
    ot_sel = acc / jnp.maximum(l, TINY)

    k_lo = pl.multiple_of(jnp.maximum(q0 - WINDOW, 0), tq)
    dist = (rel_pos(span) - k_lo).astype(f32)
    ok = jnp.abs(dist - 0.5 * (WINDOW - 1)) <= 0.5 * (WINDOW - 1)
    s = scores(kw_ref[pl.ds(k_lo, span), :], dist, ok)
    m = jnp.max(s, axis=0, keepdims=True)
    p = jnp.exp2(s - m)
    l = jnp.sum(p, axis=0, keepdims=True)
    ot_win = _dot_tn(vw_ref[pl.ds(k_lo, span), :], p.astype(bf16)) / jnp.maximum(l, TINY)

    gate = _sigmoid(gt_ref[...])
    for r in range(rep):
        oc = oc_ref[:, r * dh:(r + 1) * dh].astype(f32)
        os_ = ot_sel[:, r * tq:(r + 1) * tq].T
        ow = ot_win[:, r * tq:(r + 1) * tq].T
        o = (gate[:, r:r + 1] * oc + gate[:, rep + r:rep + r + 1] * os_
             + gate[:, 2 * rep + r:2 * rep + r + 1] * ow)
        o_ref[:, r * dh:(r + 1) * dh] = o.astype(o_ref.dtype)


def nsa_selwin_attention(qkv, sel, o_cmp, gates, tq=ATT_TQ, tk=ATT_TK):
    b, t, _ = qkv.shape
    g = NSA_KV_GROUPS
    gw = NSA_REP * NSA_HEAD_DIM
    n_s = sel.shape[2]
    assert t % tk == 0 and t % tq == 0 and WINDOW % tq == 0 and t >= WINDOW + tq
    assert tk % (SEL_BLOCK * SUBLANES) == 0 and n_s * SEL_BLOCK == t

    def kvspec(branch):
        return pl.BlockSpec((None, t, NSA_HEAD_DIM),
                            lambda i, j, k: (i, 0, NSA_HEADS + branch * g + j))

    kern = functools.partial(_selwin_kernel, tq=tq, tk=tk)
    return pl.pallas_call(
        kern,
        grid=(b, g, t // tq),
        in_specs=[pl.BlockSpec((None, tq, gw), lambda i, j, k: (i, k, j)),
                  kvspec(2), kvspec(3), kvspec(4), kvspec(5),
                  pl.BlockSpec((None, None, n_s, tq), lambda i, j, k: (i, j, 0, k)),
                  pl.BlockSpec((None, tq, gw), lambda i, j, k: (i, k, j)),
                  pl.BlockSpec((None, tq, LANES), lambda i, j, k: (i, k, j))],
        out_specs=pl.BlockSpec((None, tq, gw), lambda i, j, k: (i, k, j)),
        out_shape=jax.ShapeDtypeStruct(o_cmp.shape, bf16),
        compiler_params=_cparams("parallel", "parallel", "parallel"),
        name="nsa_selwin",
    )(qkv, qkv, qkv, qkv, qkv, sel, o_cmp, gates)


def _hgrn_tables(c):
    levels = []
    m = c // 2
    while m >= 1:
        levels.append(m)
        m //= 2
    t = np.arange(c)[:, None]
    u = np.arange(c)[None, :]
    mats = [(u <= t)]
    masks = []
    for li, m in enumerate(levels):
        ref = (t // (2 * m)) * 2 * m + m - 1
        right = (t % (2 * m)) >= m
        if li >= HGRN_BCAST_LEVELS:
            mats.append(u <= ref)
        s = np.arange(c)[None, :]
        masks.append((t // (2 * m) == s // (2 * m)) & right & ((s % (2 * m)) < m))
    lstack = np.concatenate([x.astype(np.float32) for x in mats], axis=0)
    lstack = np.concatenate([lstack, lstack], axis=1)
    return levels, lstack, np.stack([x.astype(np.float32) for x in masks])


def _hgrn_kernel(q_ref, f_ref, i_ref, g_ref, la_ref, lc_ref, oml_ref, ng_ref, ls_ref, mk_ref,
                 o_ref, expo_ref, st_ref, *, c, n_levels):
    d = HGRN_DIM

    @pl.when(pl.program_id(1) == 0)
    def _():
        st_ref[...] = jnp.zeros(st_ref.shape, f32)

    z = f_ref[...]
    log_sig = jnp.minimum(z, 0.0) - jnp.log(1.0 + jnp.exp(-jnp.abs(z)))
    y = lc_ref[...] + log_sig
    a = la_ref[...]
    log_f = jnp.maximum(a, y) + jnp.log(1.0 + jnp.exp(-jnp.abs(a - y)))
    kk = oml_ref[...] * (1.0 / (1.0 + jnp.exp(z)))
    qf = _silu(q_ref[...])
    v = i_ref[...]
    gt = _silu(g_ref[...])
    lf2 = log_f * LOG2E
    hi = lf2.astype(bf16)
    lo = (lf2 - hi.astype(f32)).astype(bf16)
    expo_ref[...] = _dot(ls_ref[...], jnp.concatenate([hi, lo], axis=0))
    ng = ng_ref[...]
    row = lax.broadcasted_iota(jnp.int32, (c, d), 0)
    for h in range(HGRN_HEADS):
        sl = slice(h * d, (h + 1) * d)
        b = expo_ref[0:c, sl]
        qh = qf[:, sl]
        kh = kk[:, sl]
        vh = v[:, sl]
        vb = vh.astype(bf16)
        st = st_ref[h]
        o = _dot_nt((qh * jnp.exp2(b)).astype(bf16), st.astype(bf16))
        amat = jnp.zeros((c, c), f32)
        for li in range(n_levels):
            m = c >> (li + 1)
            if li < HGRN_BCAST_LEVELS:
                bref = b[m - 1:m, :]
                for blk in range(1, 1 << li):
                    bref = jnp.where(row >= 2 * m * blk, b[2 * m * blk + m - 1:2 * m * blk + m, :], bref)
            else:
                k0 = (1 + li - HGRN_BCAST_LEVELS) * c
                bref = expo_ref[k0:k0 + c, sl]
            e = jnp.exp2(-jnp.abs(b - bref))
            part = _dot_nt((qh * e).astype(bf16), (kh * e).astype(bf16))
            amat = amat + mk_ref[li] * part
        diag = jnp.sum(qh * kh, axis=-1, keepdims=True)
        o = o + _dot(amat.astype(bf16), vb) + diag * vh
        b_last = b[c - 1:c, :]
        st_ref[h] = st * jnp.exp2(b_last) + _dot_tn(vb, (kh * jnp.exp2(b_last - b)).astype(bf16))
        o = o * lax.rsqrt(jnp.mean(o * o, axis=-1, keepdims=True) + EPS) * ng
        o_ref[:, sl] = (o * gt[:, sl]).astype(o_ref.dtype)


def hgrn2(proj, col0, log_lb, log1m_lb, one_m_lb, norm_g, c=HGRN_CHUNK):
    b, t, _ = proj.shape
    hd = HGRN_HEADS * HGRN_DIM
    levels, lstack, masks = _hgrn_tables(c)
    cb = col0 // hd
    assert col0 % hd == 0 and t % c == 0
    nl = len(levels)
    kern = functools.partial(_hgrn_kernel, c=c, n_levels=nl)

    def seg(k):
        return pl.BlockSpec((None, c, hd), lambda i, j: (i, j, cb + k))

    def row(width):
        return pl.BlockSpec((1, width), lambda i, j: (0, 0))

    return pl.pallas_call(
        kern,
        grid=(b, t // c),
        in_specs=[seg(0), seg(1), seg(2), seg(3), row(hd), row(hd), row(hd), row(HGRN_DIM),
                  pl.BlockSpec(lstack.shape, lambda i, j: (0, 0)),
                  pl.BlockSpec(masks.shape, lambda i, j: (0, 0, 0))],
        out_specs=pl.BlockSpec((None, c, hd), lambda i, j: (i, j, 0)),
        out_shape=jax.ShapeDtypeStruct((b, t, hd), bf16),
        scratch_shapes=[pltpu.VMEM((lstack.shape[0], hd), f32),
                        pltpu.VMEM((HGRN_HEADS, HGRN_DIM, HGRN_DIM), f32)],
        compiler_params=_cparams("parallel", "arbitrary"),
        name="hgrn2",
    )(proj, proj, proj, proj, log_lb, log1m_lb, one_m_lb, norm_g,
      jnp.asarray(lstack, bf16), jnp.asarray(masks, f32))


def _conv_kernel(x_ref, w_ref, b_ref, g_ref, bb_ref, o_ref, buf_ref, acc_ref, *, tt, sub):
    halo = CONV_HALO
    ch = x_ref.shape[-1]

    @pl.when(pl.program_id(1) == 0)
    def _():
        buf_ref[0:halo, :] = jnp.zeros((halo, ch), f32)

    buf_ref[halo:halo + tt, :] = x_ref[...]
    base = halo - (CONV_WIDTH - 1)

    def lane_block(cb, carry):
        c0 = pl.multiple_of(cb * LANES, LANES)
        w = w_ref[:, pl.ds(c0, LANES)]
        for ts in range(tt // sub):
            acc = jnp.zeros((sub, LANES), f32) + b_ref[:, pl.ds(c0, LANES)]
            for phase in range(SUBLANES):
                taps = [j for j in range(CONV_WIDTH) if (base + j) % SUBLANES == phase]
                reach = max((base + j) // SUBLANES for j in taps) * SUBLANES
                if phase == 0:
                    slab = buf_ref[pl.ds(ts * sub, sub + reach), pl.ds(c0, LANES)]
                else:
                    rows = sub + reach + SUBLANES
                    slab = pltpu.roll(buf_ref[pl.ds(ts * sub, rows), pl.ds(c0, LANES)], rows - phase, 0)
                for j in taps:
                    off = (base + j) // SUBLANES * SUBLANES
                    acc = acc + w[j:j + 1, :] * slab[off:off + sub, :]
            acc_ref[ts * sub:(ts + 1) * sub, pl.ds(c0, LANES)] = acc
        return carry

    lax.fori_loop(0, ch // LANES, lane_block, 0)
    buf_ref[0:halo, :] = buf_ref[tt:tt + halo, :]
    y = acc_ref[...]
    mu = jnp.mean(y, axis=-1, keepdims=True)
    yc = y - mu
    var = jnp.mean(yc * yc, axis=-1, keepdims=True)
    yn = yc * lax.rsqrt(var + EPS) * g_ref[...] + bb_ref[...]
    o_ref[...] = _silu(yn).astype(o_ref.dtype)


def conv_ln_swish(x, w_dw, b_dw, ln_g, ln_b, tt=CONV_TT, sub=64):
    b, t, ch = x.shape
    tt = min(tt, t)
    assert t % tt == 0 and tt % sub == 0 and tt >= CONV_HALO
    wpad = jnp.zeros((CONV_HALO, ch), f32).at[:CONV_WIDTH].set(w_dw)
    kern = functools.partial(_conv_kernel, tt=tt, sub=sub)

    def row():
        return pl.BlockSpec((1, ch), lambda i, j: (0, 0))

    return pl.pallas_call(
        kern,
        grid=(b, t // tt),
        in_specs=[pl.BlockSpec((None, tt, ch), lambda i, j: (i, j, 0)),
                  pl.BlockSpec((CONV_HALO, ch), lambda i, j: (0, 0)), row(), row(), row()],
        out_specs=pl.BlockSpec((None, tt, ch), lambda i, j: (i, j, 0)),
        out_shape=jax.ShapeDtypeStruct((b, t, ch), bf16),
        scratch_shapes=[pltpu.VMEM((tt + CONV_HALO, ch), f32), pltpu.VMEM((tt, ch), f32)],
        compiler_params=_cparams("parallel", "arbitrary"),
        name="conv_ln_swish",
    )(x, wpad, b_dw.reshape(1, ch), ln_g.reshape(1, ch), ln_b.reshape(1, ch))


def _router_kernel(x_ref, g_ref, w_ref, tri_ref, u_ref, meta_ref, cnt_ref, run_ref):
    @pl.when(pl.program_id(0) == 0)
    def _():
        run_ref[...] = jnp.zeros(run_ref.shape, f32)

    x = x_ref[...]
    u = x * lax.rsqrt(jnp.mean(x * x, axis=-1, keepdims=True) + EPS) * g_ref[...]
    u_ref[...] = u.astype(u_ref.dtype)
    uh, um, _ = _split3(u)
    wh, wm, _ = _split3(w_ref[...])
    logits = _dot(uh, wh) + (_dot(uh, wm) + _dot(um, wh))
    lane = lax.broadcasted_iota(jnp.int32, (1, LANES), 1)
    lane_f = lane.astype(f32)
    l1 = jnp.where(lane < N_EXPERTS, logits, LOWEST)
    m1 = jnp.max(l1, axis=-1, keepdims=True)
    i1 = jnp.min(jnp.where(l1 == m1, lane_f, 1e9), axis=-1, keepdims=True)
    l2 = jnp.where(lane_f == i1, LOWEST, l1)
    m2 = jnp.max(l2, axis=-1, keepdims=True)
    i2 = jnp.min(jnp.where(l2 == m2, lane_f, 1e9), axis=-1, keepdims=True)
    e2 = jnp.exp(m2 - m1)
    g1 = 1.0 / (1.0 + e2)
    g2 = e2 / (1.0 + e2)
    hit1 = lane_f == i1
    hit2 = lane_f == i2
    onehot = jnp.where(hit1 | hit2, 1.0, 0.0)
    cum = _dot(tri_ref[...], onehot.astype(bf16)) + run_ref[...]
    r1 = jnp.sum(jnp.where(hit1, cum, 0.0), axis=-1, keepdims=True)
    r2 = jnp.sum(jnp.where(hit2, cum, 0.0), axis=-1, keepdims=True)
    run = run_ref[...] + jnp.sum(onehot, axis=0, keepdims=True)
    run_ref[...] = run
    cnt_ref[...] = run
    meta = jnp.zeros(meta_ref.shape, f32)
    for col, val in enumerate((i1, i2, g1, g2, r1, r2)):
        meta = jnp.where(lane == col, val, meta)
    meta_ref[...] = meta


def moe_router(h, norm_g, w_router, tm=ROUTER_TM):
    m, d = h.shape
    tm = min(tm, m)
    wpad = jnp.zeros((d, LANES), f32).at[:, :N_EXPERTS].set(w_router)
    tri = jnp.asarray(np.tril(np.ones((tm, tm), np.float32), -1), bf16)
    return pl.pallas_call(
        _router_kernel,
        grid=(m // tm,),
        in_specs=[pl.BlockSpec((tm, d), lambda i: (i, 0)),
                  pl.BlockSpec((1, d), lambda i: (0, 0)),
                  pl.BlockSpec((d, LANES), lambda i: (0, 0)),
                  pl.BlockSpec((tm, tm), lambda i: (0, 0))],
        out_specs=[pl.BlockSpec((tm, d), lambda i: (i, 0)),
                   pl.BlockSpec((tm, LANES), lambda i: (i, 0)),
                   pl.BlockSpec((1, LANES), lambda i: (0, 0))],
        out_shape=[jax.ShapeDtypeStruct((m, d), bf16),
                   jax.ShapeDtypeStruct((m, LANES), f32),
                   jax.ShapeDtypeStruct((1, LANES), f32)],
        scratch_shapes=[pltpu.VMEM((1, LANES), f32)],
        compiler_params=_cparams("arbitrary"),
        name="moe_router",
    )(h, norm_g.reshape(1, d), wpad, tri)


def _dispatch_kernel(dest_ref, u_ref, xs_ref, zero_ref, sem, zsem, *, tm):
    @pl.when(pl.program_id(0) == 0)
    def _():
        zero_ref[...] = jnp.zeros(zero_ref.shape, zero_ref.dtype)
        zr = zero_ref.shape[0]
        nz = xs_ref.shape[0] // zr

        def fill(c, carry):
            pltpu.make_async_copy(zero_ref, xs_ref.at[pl.ds(c * zr, zr)], zsem).start()
            return carry

        lax.fori_loop(0, nz, fill, 0)

        def fill_done(c, carry):
            pltpu.make_async_copy(zero_ref, xs_ref.at[pl.ds(0, zr)], zsem).wait()
            return carry

        lax.fori_loop(0, nz, fill_done, 0)

    def issue(r, carry):
        for k in range(2):
            pltpu.make_async_copy(u_ref.at[r], xs_ref.at[dest_ref[0, 2 * r + k]], sem).start(priority=k)
        return carry

    lax.fori_loop(0, tm, issue, 0)

    def drain(r, carry):
        for k in range(2):
            pltpu.make_async_copy(u_ref.at[0], xs_ref.at[0], sem).wait()
        return carry

    lax.fori_loop(0, tm, drain, 0)


def moe_dispatch(u3, dest, p_rows, tm=ROW_TILE):
    n, s, _ = u3.shape
    tm = min(tm, n)
    zr = _largest_tile(p_rows, 512, 1)
    kern = functools.partial(_dispatch_kernel, tm=tm)
    return pl.pallas_call(
        kern,
        grid=(n // tm,),
        in_specs=[pl.BlockSpec((None, 1, 2 * tm), lambda i: (i, 0, 0), memory_space=pltpu.SMEM),
                  pl.BlockSpec((tm, s, LANES), lambda i: (i, 0, 0))],
        out_specs=pl.BlockSpec(memory_space=pl.ANY),
        out_shape=jax.ShapeDtypeStruct((p_rows, s, LANES), u3.dtype),
        scratch_shapes=[pltpu.VMEM((zr, s, LANES), u3.dtype), pltpu.SemaphoreType.DMA(()),
                        pltpu.SemaphoreType.DMA(())],
        compiler_params=_cparams("arbitrary"),
        name="moe_dispatch",
    )(dest.reshape(n // tm, 1, 2 * tm), u3)


def _combine_kernel(dest_ref, gate_ref, y_ref, h_ref, o_ref, buf_ref, sem, *, tm):
    def issue(r, carry):
        for k in range(2):
            pltpu.make_async_copy(y_ref.at[dest_ref[0, 2 * r + k]], buf_ref.at[k, r], sem).start(priority=k)
        return carry

    lax.fori_loop(0, tm, issue, 0)

    def drain(r, carry):
        for k in range(2):
            pltpu.make_async_copy(y_ref.at[0], buf_ref.at[0, 0], sem).wait()
        return carry

    lax.fori_loop(0, tm, drain, 0)

    def mix(r, carry):
        o_ref[r] = (h_ref[r] + gate_ref[0, 2 * r] * buf_ref[0, r]
                    + gate_ref[0, 2 * r + 1] * buf_ref[1, r])
        return carry

    lax.fori_loop(0, tm, mix, 0)


def moe_combine(y3, dest, gate, h3, tm=ROW_TILE):
    n, s, _ = h3.shape
    tm = min(tm, n)
    kern = functools.partial(_combine_kernel, tm=tm)
    return pl.pallas_call(
        kern,
        grid=(n // tm,),
        in_specs=[pl.BlockSpec((None, 1, 2 * tm), lambda i: (i, 0, 0), memory_space=pltpu.SMEM),
                  pl.BlockSpec((None, 1, 2 * tm), lambda i: (i, 0, 0), memory_space=pltpu.SMEM),
                  pl.BlockSpec(memory_space=pl.ANY),
                  pl.BlockSpec((tm, s, LANES), lambda i: (i, 0, 0))],
        out_specs=pl.BlockSpec((tm, s, LANES), lambda i: (i, 0, 0)),
        out_shape=jax.ShapeDtypeStruct(h3.shape, f32),
        scratch_shapes=[pltpu.VMEM((2, tm, s, LANES), f32), pltpu.SemaphoreType.DMA(())],
        compiler_params=_cparams("arbitrary"),
        name="moe_combine",
    )(dest.reshape(n // tm, 1, 2 * tm), gate.reshape(n // tm, 1, 2 * tm), y3, h3)


def _mixer_ab(h2, bsz, t, norm_g, w_in, w_out, pos_k, pos_v, w1_k, w1_v, w2_k, w2_v, lb, hgrn_norm):
    m, d = h2.shape
    nq = NSA_HEADS * NSA_HEAD_DIM
    nkv = NSA_KV_GROUPS * NSA_HEAD_DIM
    hd = HGRN_HEADS * HGRN_DIM
    g, rep = NSA_KV_GROUPS, NSA_REP
    u = rmsnorm(h2, norm_g, bf16)
    o_kv = nq
    o_gate = o_kv + 6 * nkv
    o_hg = o_gate + 3 * NSA_HEADS
    w_attn = w_in[:, :o_gate].astype(bf16)[None]
    w_hg = w_in[:, o_hg:o_hg + 4 * hd].astype(bf16)[None]
    wg = w_in[:, o_gate:o_hg].reshape(d, g, rep, 3).transpose(0, 1, 3, 2).reshape(d, g, 3 * rep)
    wg = jnp.pad(wg, ((0, 0), (0, 0), (0, LANES - 3 * rep))).reshape(d, g * LANES).astype(bf16)[None]

    attn = matmul(u, w_attn, out_dtype=bf16)
    hg = matmul(u, w_hg, out_dtype=f32, tn=1024)
    gates = matmul(u, wg, out_dtype=f32, tn=g * LANES)

    qkv = attn.reshape(bsz, t, nq + 6 * nkv)
    n_c = (t - CMP_BLOCK) // CMP_STRIDE + 1
    n_s = t // SEL_BLOCK
    rows = t // CMP_STRIDE
    xg = qkv[:, :, nq:nq + 2 * nkv].reshape(bsz, t, 2, g, NSA_HEAD_DIM).transpose(0, 2, 3, 1, 4)
    xg = xg.reshape(bsz, 2, g, rows, CMP_STRIDE * NSA_HEAD_DIM)
    pos = jnp.stack([pos_k, pos_v]).reshape(2, 1, CMP_BLOCK * NSA_HEAD_DIM)
    kvc = nsa_compress(xg, pos, jnp.stack([w1_k, w1_v]).astype(bf16),
                       jnp.stack([w2_k, w2_v]).astype(bf16), n_c)
    cj = np.arange(LANES)[:, None] * CMP_STRIDE
    sk = np.arange(LANES)[None, :] * SEL_BLOCK
    overlap = ((cj < sk + SEL_BLOCK) & (cj + CMP_BLOCK > sk)
               & (np.arange(LANES)[:, None] < n_c) & (np.arange(LANES)[None, :] < n_s))
    o_cmp, sel = nsa_cmp_attention(qkv, kvc, jnp.asarray(overlap.astype(np.float32), bf16), n_c, n_s)
    o_a = nsa_selwin_attention(qkv, sel, o_cmp, gates.reshape(bsz, t, g * LANES))

    lbc = jnp.clip(lb.astype(f32), 0.0, LB_MAX).reshape(1, hd)
    o_b = hgrn2(hg.reshape(bsz, t, 4 * hd), 0, jnp.log(jnp.maximum(lbc, TINY)), jnp.log1p(-lbc),
                1.0 - lbc, hgrn_norm.reshape(1, HGRN_DIM))
    mix = jnp.concatenate([o_a.reshape(m, nq), o_b.reshape(m, hd)], axis=-1)
    return matmul(mix, w_out.astype(bf16)[None], residual=h2, out_dtype=f32, tn=1024)


def _dense_ffn(h2, norm_g, w1_all, w3_all, w2, j, tm=1024):
    u = rmsnorm(h2, norm_g, bf16)
    tm = _largest_tile(h2.shape[0], tm, SUBLANES)
    nblk = h2.shape[0] // tm
    mid = matmul_gated(u, w1_all, w3_all, act_a="silu", tm=tm, n_outer=True,
                       blk_e=jnp.full((nblk,), j, jnp.int32), n_used=jnp.full((1,), nblk, jnp.int32))
    return matmul(mid, w2.astype(bf16)[None], residual=h2, out_dtype=f32)


def _conformer(h2, bsz, t, norm_g, w_pw1, b_pw1, w_dw, b_dw, ln_g, ln_b, w_pw2, b_pw2):
    m, d = h2.shape
    ch = w_dw.shape[-1]
    u = rmsnorm(h2, norm_g, bf16)
    glu = matmul_gated(u, w_pw1[:, :ch].astype(bf16)[None], w_pw1[:, ch:].astype(bf16)[None],
                       ba=b_pw1[:ch].reshape(1, ch), bb=b_pw1[ch:].reshape(1, ch),
                       act_b="sigmoid", out_dtype=f32, tn=1024)
    y = conv_ln_swish(glu.reshape(bsz, t, ch), w_dw, b_dw, ln_g, ln_b)
    return matmul(y.reshape(m, ch), w_pw2.astype(bf16)[None], bias=b_pw2.reshape(1, d),
                  residual=h2, out_dtype=f32, tn=1024)


def _moe(h2, norm_g, w_router, w1_all, w3_all, w2_all, j, tm=MOE_TM):
    n, d = h2.shape
    e = N_EXPERTS
    s = d // LANES
    u, meta, counts = moe_router(h2, norm_g, w_router)
    tm = min(tm, n)
    idx = meta[:, 0:2].astype(jnp.int32)
    gate = meta[:, 2:4]
    rank = meta[:, 4:6].astype(jnp.int32)
    cnt = counts[0, :e].astype(jnp.int32)
    nblk = (cnt + tm - 1) // tm
    blk_end = jnp.cumsum(nblk)
    pstart = (blk_end - nblk) * tm
    dest = pstart[idx] + rank
    total_blk = (2 * n) // tm + e
    p_rows = total_blk * tm
    blk_e = jnp.minimum(jnp.searchsorted(blk_end, jnp.arange(total_blk), side="right"), e - 1)
    blk_e = blk_e.astype(jnp.int32)
    n_used = blk_end[-1:].astype(jnp.int32)

    xs = moe_dispatch(u.reshape(n, s, LANES), dest, p_rows).reshape(p_rows, d)
    mid = matmul_gated(xs, w1_all, w3_all, act_a="silu", tm=tm, tn=1024, n_outer=True,
                       blk_e=blk_e + j * e, n_used=n_used)
    y = matmul(mid, w2_all, out_dtype=f32, tm=tm, blk_e=blk_e + j * e, n_used=n_used, n_outer=True)
    out = moe_combine(y.reshape(p_rows, s, LANES), dest, gate, h2.reshape(n, s, LANES))
    return out.reshape(n, d)


def _lower_bounds(table):
    p = jax.nn.softmax(table.astype(f32), axis=0)
    return jnp.cumsum(p, axis=0) - p[0]


def kernel(x, norm_mix, norm_ffn, final_norm, ab_w_in, ab_w_out, cmp_pos_k, cmp_pos_v, cmp_w1_k, cmp_w1_v, cmp_w2_k, cmp_w2_v, hgrn_lower_bounds, hgrn_norm, ffn_w1, ffn_w3, ffn_w2, conv_w_pw1, conv_b_pw1, conv_w_dw, conv_b_dw, conv_ln_g, conv_ln_b, conv_w_pw2, conv_b_pw2, moe_router, moe_w1, moe_w3, moe_w2):
    bsz, t, d = x.shape
    depth = norm_mix.shape[0]
    lbs = _lower_bounds(hgrn_lower_bounds)
    h = x.reshape(bsz * t, d)
    for layer in range(depth):
        j = layer // 2
        if layer % 2 == 0:
            h = _mixer_ab(h, bsz, t, norm_mix[layer], ab_w_in[j], ab_w_out[j], cmp_pos_k[j],
                          cmp_pos_v[j], cmp_w1_k[j], cmp_w1_v[j], cmp_w2_k[j], cmp_w2_v[j],
                          lbs[j], hgrn_norm[j])
            h = _dense_ffn(h, norm_ffn[layer], ffn_w1, ffn_w3, ffn_w2[j], j)
        else:
            h = _conformer(h, bsz, t, norm_mix[layer], conv_w_pw1[j], conv_b_pw1[j], conv_w_dw[j],
                           conv_b_dw[j], conv_ln_g[j], conv_ln_b[j], conv_w_pw2[j], conv_b_pw2[j])
            fe = moe_w1.shape[-1]
            h = _moe(h, norm_ffn[layer], moe_router[j], moe_w1.reshape(-1, d, fe),
                     moe_w3.reshape(-1, d, fe), moe_w2.reshape(-1, fe, d), j)
    return rmsnorm(h, final_norm, f32).reshape(bsz, t, d)
```
